```python
import jax
import jax.numpy as jnp
from jax import lax
import numpy as np

D_MODEL = 2048
BATCH = 2
SEQ = 8192
DEPTH = 2

GRID_W = 64
CTX_LEN = 256
EPS = 1e-6

N_EVEN = (DEPTH + 1) // 2
N_ODD = DEPTH // 2

A_WIDTH = D_MODEL // 2
A_HEAD_DIM = 128
A_HEADS = A_WIDTH // A_HEAD_DIM
A_CHUNK = 32
B_WIDTH = D_MODEL // 2
POOL_WINDOWS = (2, 4, 8, 16)
B_GROUPS = len(POOL_WINDOWS)
B_GROUP_DIM = B_WIDTH // B_GROUPS
EVEN_IN = 5 * A_WIDTH + B_WIDTH
EVEN_MIX = A_WIDTH + B_WIDTH

C_HEAD_DIM = 128
C_KEY_HEADS = D_MODEL // C_HEAD_DIM
C_VALUE_HEADS = 2 * C_KEY_HEADS
C_KEY_WIDTH = C_KEY_HEADS * C_HEAD_DIM
C_VALUE_WIDTH = C_VALUE_HEADS * C_HEAD_DIM
C_QKV = 2 * C_KEY_WIDTH + C_VALUE_WIDTH
C_CONV = 4
C_CHUNK = 64
ODD_IN = C_QKV + C_VALUE_WIDTH + 4 * C_VALUE_HEADS

FFN_HIDDEN = -(-8 * D_MODEL // (3 * 256)) * 256

kernel_name = "hybrid_hgrn2_pool_gdn_diffusion_trunk"


def rmsnorm(x, w):
    xf = x.astype(jnp.float32)
    y = xf * lax.rsqrt(jnp.mean(xf * xf, axis=-1, keepdims=True) + EPS)
    return (y * w.astype(jnp.float32)).astype(x.dtype)


def l2norm(x):
    return x * lax.rsqrt(jnp.sum(x * x, axis=-1, keepdims=True) + EPS)


def modulate(h, shift, scale):
    return h * (1.0 + scale) + shift


def split_heads(t, d):
    return t.reshape(t.shape[:-1] + (t.shape[-1] // d, d))


def swiglu(h, w13, w2):
    gate, up = jnp.split(h @ w13, 2, axis=-1)
    return (jax.nn.silu(gate) * up) @ w2


def centred_short_conv(u, w):
    length = u.shape[1]
    left = C_CONV // 2
    up = jnp.pad(u, ((0, 0), (left, C_CONV - 1 - left), (0, 0)))
    out = up[:, 0:length] * w[0]
    for j in range(1, C_CONV):
        out = out + up[:, j:j + length] * w[j]
    return out


def hgrn2_chunk_scan(q, k, v, log_f, s0):
    bsz, length, heads, _ = q.shape
    dv = v.shape[-1]
    n = length // A_CHUNK
    rs = lambda t: t.reshape(bsz, n, A_CHUNK, heads, t.shape[-1]).transpose(1, 0, 3, 2, 4)
    q, k, v, log_f = rs(q), rs(k), rs(v), rs(log_f)
    b = jnp.cumsum(log_f, axis=-2)
    b_last = b[..., -1:, :]
    b_mid = b[..., A_CHUNK // 2 - 1:A_CHUNK // 2, :]
    causal = jnp.tril(jnp.ones((A_CHUNK, A_CHUNK), bool))
    scores = jnp.einsum('nbhtd,nbhsd->nbhts', q * jnp.exp(b - b_mid), k * jnp.exp(b_mid - b))
    o_intra = jnp.einsum('nbhts,nbhsv->nbhtv', jnp.where(causal, scores, 0.0), v)
    q_in = q * jnp.exp(b)
    k_out = k * jnp.exp(b_last - b)
    decay = jnp.exp(b_last)

    def step(state, inp):
        qi, ki, vi, dl = inp
        o = jnp.einsum('bhtd,bhdv->bhtv', qi, state)
        state = state * dl[..., 0, :, None] + jnp.einsum('bhsd,bhsv->bhdv', ki, vi)
        return state, o

    s_final, o_inter = lax.scan(step, s0, (q_in, k_out, v, decay))
    o = (o_intra + o_inter).transpose(1, 0, 3, 2, 4).reshape(bsz, length, heads, dv)
    return o, s_final


def gated_delta_chunk_scan(q, k, v, g, beta, s0):
    bsz, length, heads, _ = q.shape
    dv = v.shape[-1]
    n = length // C_CHUNK
    rs = lambda t: t.reshape(bsz, n, C_CHUNK, heads, t.shape[-1]).transpose(1, 0, 3, 2, 4)
    q, k, v = rs(q), rs(k), rs(v)
    g = rs(g[..., None])[..., 0]
    beta = rs(beta[..., None])[..., 0]
    gc = jnp.cumsum(g, axis=-1)
    causal = jnp.tril(jnp.ones((C_CHUNK, C_CHUNK), bool))
    strict = jnp.tril(jnp.ones((C_CHUNK, C_CHUNK), bool), -1)
    diff = gc[..., :, None] - gc[..., None, :]
    decay = jnp.where(causal, jnp.exp(jnp.where(causal, diff, 0.0)), 0.0)
    k_beta = k * beta[..., None]
    v_beta = v * beta[..., None]
    a_low = jnp.where(strict, jnp.einsum('nbhtd,nbhsd->nbhts', k_beta, k) * decay, 0.0)
    t_mat = jnp.eye(C_CHUNK, dtype=a_low.dtype) + a_low
    u = lax.linalg.triangular_solve(t_mat, v_beta, left_side=True, lower=True, unit_diagonal=True)
    w = lax.linalg.triangular_solve(t_mat, k_beta * jnp.exp(gc)[..., None], left_side=True,
                                    lower=True, unit_diagonal=True)
    qk = jnp.where(causal, jnp.einsum('nbhtd,nbhsd->nbhts', q, k) * decay, 0.0)
    q_dec = q * jnp.exp(gc)[..., None]
    k_dec = k * jnp.exp(gc[..., -1:] - gc)[..., None]
    g_last = jnp.exp(gc[..., -1])

    def step(state, inp):
        u_i, w_i, qk_i, qd_i, kd_i, gl_i = inp
        v_new = u_i - jnp.einsum('bhtd,bhdv->bhtv', w_i, state)
        o = jnp.einsum('bhtd,bhdv->bhtv', qd_i, state) + jnp.einsum('bhts,bhsv->bhtv', qk_i, v_new)
        state = state * gl_i[..., None, None] + jnp.einsum('bhsd,bhsv->bhdv', kd_i, v_new)
        return state, o

    s_final, o = lax.scan(step, s0, (u, w, qk, q_dec, k_dec, g_last))
    o = o.transpose(1, 0, 3, 2, 4).reshape(bsz, length, heads, dv)
    return o, s_final


def bidirectional_prefix_scan(scan_fn, ctx_f, lat_f, ctx_b, lat_b, s0):
    flip = lambda args: tuple(jnp.flip(t, axis=1) for t in args)
    oc_f, sc_f = scan_fn(*ctx_f, s0)
    ol_f, _ = scan_fn(*lat_f, sc_f)
    oc_b, sc_b = scan_fn(*flip(ctx_b), s0)
    ol_b, _ = scan_fn(*flip(lat_b), sc_b)
    return ol_f + jnp.flip(ol_b, axis=1), oc_f + jnp.flip(oc_b, axis=1)


def multiscale_pool(u, pool_w, pool_scale):
    length = u.shape[-2]
    uf = u.astype(jnp.float32).reshape(u.shape[:-1] + (B_GROUPS, B_GROUP_DIM))
    cs = jnp.concatenate([jnp.zeros_like(uf[..., :1, :, :]), jnp.cumsum(uf, axis=-3)], axis=-3)
    pos = jnp.arange(length)
    mixed = []
    for gi, win in enumerate(POOL_WINDOWS):
        lo = jnp.clip(pos - win // 2, 0, length - 1)
        hi = jnp.clip(pos + win - 1 - win // 2, 0, length - 1)
        cnt = (hi - lo + 1).astype(jnp.float32)[:, None]
        csg = cs[..., gi, :]
        mean = (jnp.take(csg, hi + 1, axis=-2) - jnp.take(csg, lo, axis=-2)) / cnt
        mixed.append(mean - uf[..., gi, :])
    d = jnp.stack(mixed, axis=-2)
    y = jnp.einsum('...lgc,gcd->...lgd', d, pool_w.astype(jnp.float32))
    return y.reshape(u.shape) * pool_scale.astype(jnp.float32)


def hgrn2_gates(pre_f, lb):
    s = pre_f.astype(jnp.float32)
    log_f = jnp.log(lb + (1.0 - lb) * jax.nn.sigmoid(s))
    k = (1.0 - lb) * jax.nn.sigmoid(-s)
    return split_heads(k, A_HEAD_DIM), split_heads(log_f, A_HEAD_DIM)


def even_mixer(h_lat, h_ctx, rows, lb, w_in, a_norm, pool_w, pool_scale, w_out, need_ctx):
    bsz, seq_len, _ = h_lat.shape

    def project(h):
        p = h @ w_in
        q, f_f, f_b, i, g, u = jnp.split(
            p, [A_WIDTH, 2 * A_WIDTH, 3 * A_WIDTH, 4 * A_WIDTH, 5 * A_WIDTH], axis=-1)
        q = split_heads(jax.nn.silu(q.astype(jnp.float32)), A_HEAD_DIM)
        i = split_heads(i.astype(jnp.float32), A_HEAD_DIM)
        k_f, logf_f = hgrn2_gates(f_f, lb[0])
        k_b, logf_b = hgrn2_gates(f_b, lb[1])
        return (q, k_f, i, logf_f), (q, k_b, i, logf_b), g, u

    def readout(h, o, g, pooled):
        a_out = rmsnorm(o, a_norm) * jax.nn.silu(split_heads(g.astype(jnp.float32), A_HEAD_DIM))
        a_out = a_out.reshape(o.shape[:-2] + (A_WIDTH,))
        return jnp.concatenate([a_out, pooled], axis=-1).astype(h.dtype) @ w_out

    fwd_l, bwd_l, g_l, u_l = project(h_lat)
    fwd_c, bwd_c, g_c, u_c = project(h_ctx)
    s0 = jnp.zeros((bsz, A_HEADS, A_HEAD_DIM, A_HEAD_DIM), jnp.float32)
    o_l, o_c = bidirectional_prefix_scan(hgrn2_chunk_scan, fwd_c, fwd_l, bwd_c, bwd_l, s0)
    pooled_l = multiscale_pool(u_l.reshape(bsz, rows, GRID_W, B_WIDTH), pool_w, pool_scale)
    y_l = readout(h_lat, o_l, g_l, pooled_l.reshape(bsz, seq_len, B_WIDTH))
    if not need_ctx:
        return y_l, None
    y_c = readout(h_ctx, o_c, g_c, multiscale_pool(u_c, pool_w, pool_scale))
    return y_l, y_c


def odd_mixer(h_lat, h_ctx, w_in, conv_w, a_log, dt_bias, norm_w, w_out, need_ctx):
    bsz = h_lat.shape[0]
    rep = C_VALUE_HEADS // C_KEY_HEADS
    a_rate = jnp.exp(a_log.astype(jnp.float32))
    dtb = dt_bias.astype(jnp.float32)

    def project(h):
        p = h @ w_in
        qkv, z, gates = jnp.split(p, [C_QKV, C_QKV + C_VALUE_WIDTH], axis=-1)
        qkv = jax.nn.silu(centred_short_conv(qkv, conv_w).astype(jnp.float32))
        q, k, v = jnp.split(qkv, [C_KEY_WIDTH, 2 * C_KEY_WIDTH], axis=-1)
        q = jnp.repeat(l2norm(split_heads(q, C_HEAD_DIM)) * C_HEAD_DIM ** -0.5, rep, axis=2)
        k = jnp.repeat(l2norm(split_heads(k, C_HEAD_DIM)), rep, axis=2)
        v = split_heads(v, C_HEAD_DIM)
        a_f, a_b, b_f, b_b = jnp.split(gates.astype(jnp.float32), 4, axis=-1)
        g_f = -a_rate[0] * jax.nn.softplus(a_f + dtb[0])
        g_b = -a_rate[1] * jax.nn.softplus(a_b + dtb[1])
        return (q, k, v, g_f, jax.nn.sigmoid(b_f)), (q, k, v, g_b, jax.nn.sigmoid(b_b)), z

    def readout(h, o, z):
        y = rmsnorm(o, norm_w) * jax.nn.silu(split_heads(z.astype(jnp.float32), C_HEAD_DIM))
        return y.reshape(o.shape[:-2] + (C_VALUE_WIDTH,)).astype(h.dtype) @ w_out

    fwd_l, bwd_l, z_l = project(h_lat)
    fwd_c, bwd_c, z_c = project(h_ctx)
    s0 = jnp.zeros((bsz, C_VALUE_HEADS, C_HEAD_DIM, C_HEAD_DIM), jnp.float32)
    o_l, o_c = bidirectional_prefix_scan(gated_delta_chunk_scan, fwd_c, fwd_l, bwd_c, bwd_l, s0)
    y_l = readout(h_lat, o_l, z_l)
    if not need_ctx:
        return y_l, None
    return y_l, readout(h_ctx, o_c, z_c)


def setup_inputs(seed: int = 0) -> dict:
    key = jax.random.key(seed)
    ks = jax.random.split(key, 24)
    nrm = lambda k, shape, s: jax.random.normal(k, shape, jnp.float32) * s
    dt = jnp.exp(jax.random.uniform(ks[15], (N_ODD, 2, C_VALUE_HEADS), jnp.float32,
                                    minval=float(np.log(1e-3)), maxval=float(np.log(1e-1))))
    return {
        "x": nrm(ks[0], (BATCH, SEQ, D_MODEL), 1.0),
        "c": nrm(ks[1], (BATCH, D_MODEL), 1.0),
        "ctx": nrm(ks[2], (BATCH, CTX_LEN, D_MODEL), 1.0),
        "c_ctx": nrm(ks[3], (D_MODEL,), 1.0),
        "w_ada": nrm(ks[4], (DEPTH, D_MODEL, 6 * D_MODEL), 0.5 * D_MODEL ** -0.5),
        "b_ada": nrm(ks[5], (DEPTH, 6 * D_MODEL), 0.02),
        "norm_w": 1.0 + nrm(ks[6], (DEPTH, 4, D_MODEL), 0.02),
        "ev_w_in": nrm(ks[7], (N_EVEN, D_MODEL, EVEN_IN), D_MODEL ** -0.5),
        "ev_lb": nrm(ks[8], (2, DEPTH + 1, A_WIDTH), 0.1),
        "ev_a_norm": 1.0 + nrm(ks[9], (N_EVEN, A_HEAD_DIM), 0.02),
        "ev_pool_w": nrm(ks[10], (N_EVEN, B_GROUPS, B_GROUP_DIM, B_GROUP_DIM), B_GROUP_DIM ** -0.5),
        "ev_pool_scale": 1.0 + nrm(ks[11], (N_EVEN, B_WIDTH), 0.02),
        "ev_w_out": nrm(ks[12], (N_EVEN, EVEN_MIX, D_MODEL), EVEN_MIX ** -0.5),
        "od_w_in": nrm(ks[13], (N_ODD, D_MODEL, ODD_IN), D_MODEL ** -0.5),
        "od_conv": nrm(ks[14], (N_ODD, C_CONV, C_QKV), C_CONV ** -0.5),
        "od_A_log": jnp.log(jax.random.uniform(ks[16], (N_ODD, 2, C_VALUE_HEADS), jnp.float32,
                                               minval=1.0, maxval=16.0)),
        "od_dt_bias": dt + jnp.log(-jnp.expm1(-dt)),
        "od_norm": 1.0 + nrm(ks[17], (N_ODD, C_HEAD_DIM), 0.02),
        "od_w_out": nrm(ks[18], (N_ODD, C_VALUE_WIDTH, D_MODEL), C_VALUE_WIDTH ** -0.5),
        "ffn_w13": nrm(ks[19], (DEPTH, D_MODEL, 2 * FFN_HIDDEN), D_MODEL ** -0.5),
        "ffn_w2": nrm(ks[20], (DEPTH, FFN_HIDDEN, D_MODEL), FFN_HIDDEN ** -0.5),
    }


def reference(x, c, ctx, c_ctx, w_ada, b_ada, norm_w, ev_w_in, ev_lb, ev_a_norm, ev_pool_w,
              ev_pool_scale, ev_w_out, od_w_in, od_conv, od_A_log, od_dt_bias, od_norm, od_w_out,
              ffn_w13, ffn_w2):
    rows = x.shape[1] // GRID_W
    lb_all = jnp.cumsum(jax.nn.softmax(ev_lb.astype(jnp.float32), axis=1), axis=1)
    silu_c = jax.nn.silu(c)
    silu_cc = jax.nn.silu(c_ctx)[None, :]
    for layer in range(DEPTH):
        need_ctx = layer < DEPTH - 1
        j = layer // 2
        m_l = [t[:, None, :] for t in jnp.split(silu_c @ w_ada[layer] + b_ada[layer], 6, axis=-1)]
        m_c = [t[:, None, :] for t in jnp.split(silu_cc @ w_ada[layer] + b_ada[layer], 6, axis=-1)]
        h_l = modulate(rmsnorm(x, norm_w[layer, 0]), m_l[0], m_l[1])
        h_c = modulate(rmsnorm(ctx, norm_w[layer, 0]), m_c[0], m_c[1])
        if layer % 2 == 0:
            y_l, y_c = even_mixer(h_l, h_c, rows, lb_all[:, layer], ev_w_in[j], ev_a_norm[j],
                                  ev_pool_w[j], ev_pool_scale[j], ev_w_out[j], need_ctx)
        else:
            y_l, y_c = odd_mixer(h_l, h_c, od_w_in[j], od_conv[j], od_A_log[j], od_dt_bias[j],
                                 od_norm[j], od_w_out[j], need_ctx)
        x = x + m_l[2] * rmsnorm(y_l, norm_w[layer, 1])
        f_l = swiglu(modulate(rmsnorm(x, norm_w[layer, 2]), m_l[3], m_l[4]), ffn_w13[layer], ffn_w2[layer])
        x = x + m_l[5] * rmsnorm(f_l, norm_w[layer, 3])
        if need_ctx:
            ctx = ctx + m_c[2] * rmsnorm(y_c, norm_w[layer, 1])
            f_c = swiglu(modulate(rmsnorm(ctx, norm_w[layer, 2]), m_c[3], m_c[4]), ffn_w13[layer], ffn_w2[layer])
            ctx = ctx + m_c[5] * rmsnorm(f_c, norm_w[layer, 3])
    return x
```

```python
import functools

import numpy as np
import jax
import jax.numpy as jnp
from jax import lax
from jax.experimental import pallas as pl
from jax.experimental.pallas import tpu as pltpu

F32 = jnp.float32
BF16 = jnp.bfloat16
EPS = 1e-6

HEAD_DIM = 128
GRID_W = 64
POOL_WINDOWS = (2, 4, 8, 16)
A_CHUNK = 32
C_CONV = 4
C_CHUNK = 64
VMEM_LIMIT = 56 * 1024 * 1024

NT = (((1,), (1,)), ((), ()))
TN = (((0,), (0,)), ((), ()))


def _dot(a, b):
    return jnp.dot(a, b, preferred_element_type=F32)


def _dot_nt(a, b):
    return lax.dot_general(a, b, NT, preferred_element_type=F32)


def _dot_tn(a, b):
    return lax.dot_general(a, b, TN, preferred_element_type=F32)


def _silu(x):
    return x * jax.nn.sigmoid(x)


def _rms(x):
    return x * lax.rsqrt(jnp.mean(x * x, axis=-1, keepdims=True) + EPS)


def _split_bf16(x):
    hi = x.astype(BF16)
    lo = (x - hi.astype(F32)).astype(BF16)
    return hi, lo


def _params(sem):
    return pltpu.CompilerParams(dimension_semantics=sem, vmem_limit_bytes=VMEM_LIMIT)


def _ada_kernel(c_ref, w_ref, b_ref, o_ref):
    s = _silu(c_ref[...]).astype(BF16)
    o_ref[...] = _dot(s, w_ref[...].astype(BF16)) + b_ref[...]


def _ada(cvec, w_ada, b_ada):
    depth, d, n = w_ada.shape
    tn = 1024
    return pl.pallas_call(
        _ada_kernel,
        grid=(depth, n // tn),
        in_specs=[
            pl.BlockSpec((8, d), lambda l, j: (0, 0)),
            pl.BlockSpec((None, d, tn), lambda l, j: (l, 0, j)),
            pl.BlockSpec((None, 1, tn), lambda l, j: (l, 0, j)),
        ],
        out_specs=pl.BlockSpec((None, 8, tn), lambda l, j: (l, 0, j)),
        out_shape=jax.ShapeDtypeStruct((depth, 8, n), F32),
        compiler_params=_params(("parallel", "parallel")),
        name="ada",
    )(cvec, w_ada, b_ada.reshape(depth, 1, n))


def _mod_spec(layer, row_of_batch):
    return lambda d: pl.BlockSpec((None, None, 6, d), lambda b, *_: (layer, row_of_batch(b), 0, 0))


def _proj_kernel(x_ref, mod_ref, nw_ref, w_ref, *rest, has_gate):
    if has_gate:
        wgh_ref, wgl_ref, o_ref, og_ref, h_scr = rest
    else:
        o_ref, h_scr = rest
    j = pl.program_id(2)

    @pl.when(j == 0)
    def _():
        h = _rms(x_ref[...]) * nw_ref[...]
        h = h * (1.0 + mod_ref[1:2, :]) + mod_ref[0:1, :]
        hh = h.astype(BF16)
        h_scr[...] = hh
        if has_gate:
            hl = (h - hh.astype(F32)).astype(BF16)
            og_ref[...] = _dot(hh, wgh_ref[...]) + _dot(hl, wgh_ref[...]) + _dot(hh, wgl_ref[...])

    o_ref[...] = _dot(h_scr[...], w_ref[...])


def _proj(x, mod, layer, row_of_batch, nw, w, wg=None, tm=512, tn=1024):
    bsz, length, d = x.shape
    n = w.shape[1]
    tm = min(tm, length)
    has_gate = wg is not None
    in_specs = [
        pl.BlockSpec((None, tm, d), lambda b, i, j: (b, i, 0)),
        pl.BlockSpec((None, None, 6, d), lambda b, i, j: (layer, row_of_batch(b), 0, 0)),
        pl.BlockSpec((1, d), lambda b, i, j: (0, 0)),
        pl.BlockSpec((d, tn), lambda b, i, j: (0, j)),
    ]
    out_specs = [pl.BlockSpec((None, tm, tn), lambda b, i, j: (b, i, j))]
    out_shape = [jax.ShapeDtypeStruct((bsz, length, n), F32)]
    args = [x, mod, nw.reshape(1, d), w]
    if has_gate:
        ng = wg[0].shape[1]
        in_specs += [pl.BlockSpec((d, ng), lambda b, i, j: (0, 0))] * 2
        out_specs.append(pl.BlockSpec((None, tm, ng), lambda b, i, j: (b, i, 0)))
        out_shape.append(jax.ShapeDtypeStruct((bsz, length, ng), F32))
        args += list(wg)
    res = pl.pallas_call(
        functools.partial(_proj_kernel, has_gate=has_gate),
        grid=(bsz, length // tm, n // tn),
        in_specs=in_specs,
        out_specs=out_specs,
        out_shape=out_shape,
        scratch_shapes=[pltpu.VMEM((tm, d), BF16)],
        compiler_params=_params(("parallel", "parallel", "arbitrary")),
        name="proj",
    )(*args)
    return res if has_gate else res[0]


def _hgrn_kernel(qf_ref, ff_ref, if_ref, qb_ref, fb_ref, ib_ref, lb_ref, s0_ref, trif_ref, trib_ref,
                 of_ref, ob_ref, sfin_ref, st_scr):
    n = pl.program_id(2)
    lblk = qf_ref.shape[0]
    nchunk = lblk // A_CHUNK

    @pl.when(n == 0)
    def _():
        st_scr[...] = s0_ref[...]

    dirs = ((qf_ref, ff_ref, if_ref, of_ref, trif_ref), (qb_ref, fb_ref, ib_ref, ob_ref, trib_ref))
    for d, (q_ref, f_ref, i_ref, o_ref, tri_ref) in enumerate(dirs):
        lb = lb_ref[d]
        s = f_ref[...]
        logf = jnp.log(lb + (1.0 - lb) * jax.nn.sigmoid(s))
        k = (1.0 - lb) * jax.nn.sigmoid(-s)
        qa = _silu(q_ref[...])
        v = i_ref[...].astype(BF16)
        tri = tri_ref[...]
        hi, lo = _split_bf16(logf)
        b = _dot(tri, hi) + _dot(tri, lo)
        b3 = b.reshape(nchunk, A_CHUNK, HEAD_DIM)
        mid = A_CHUNK // 2 - 1 if d == 0 else A_CHUNK // 2
        last = A_CHUNK - 1 if d == 0 else 0
        bmid = jnp.broadcast_to(b3[:, mid:mid + 1, :], b3.shape).reshape(b.shape)
        blast = jnp.broadcast_to(b3[:, last:last + 1, :], b3.shape).reshape(b.shape)
        qs = (qa * jnp.exp(b - bmid)).astype(BF16)
        ks = (k * jnp.exp(bmid - b)).astype(BF16)
        scores = jnp.where(tri > 0, _dot_nt(qs, ks), 0.0)
        o_intra = _dot(scores.astype(BF16), v)
        q_in = (qa * jnp.exp(b)).astype(BF16)
        k_out = (k * jnp.exp(blast - b)).astype(BF16)
        dec = jnp.exp(blast)
        st = st_scr[d]
        order = range(nchunk) if d == 0 else range(nchunk - 1, -1, -1)
        for c in order:
            r = slice(c * A_CHUNK, (c + 1) * A_CHUNK)
            o_ref[r, :] = o_intra[r] + _dot_nt(q_in[r], st.astype(BF16))
            st = st * dec[c * A_CHUNK:c * A_CHUNK + 1] + _dot_tn(v[r], k_out[r])
        st_scr[d] = st

    @pl.when(n == pl.num_programs(2) - 1)
    def _():
        sfin_ref[...] = st_scr[...]


def _chunk_tri(lblk, chunk, upper):
    t = np.arange(lblk)
    same = (t[:, None] // chunk) == (t[None, :] // chunk)
    tri = (t[None, :] >= t[:, None]) if upper else (t[None, :] <= t[:, None])
    return jnp.asarray(same & tri, BF16)


def _hgrn(p, lb, s0, lblk=256):
    bsz, length, _ = p.shape
    heads = lb.shape[1] // HEAD_DIM
    lblk = min(lblk, length)
    nb = length // lblk
    blk = lambda col, rev: pl.BlockSpec(
        (None, lblk, HEAD_DIM),
        (lambda b, h, n: (b, nb - 1 - n, col * heads + h)) if rev else (lambda b, h, n: (b, n, col * heads + h)))
    oblk = lambda rev: pl.BlockSpec(
        (None, lblk, HEAD_DIM), (lambda b, h, n: (b, nb - 1 - n, h)) if rev else (lambda b, h, n: (b, n, h)))
    st_spec = pl.BlockSpec((2, None, None, HEAD_DIM, HEAD_DIM), lambda b, h, n: (0, b, h, 0, 0))
    tri_spec = pl.BlockSpec((lblk, lblk), lambda b, h, n: (0, 0))
    width = heads * HEAD_DIM
    return pl.pallas_call(
        _hgrn_kernel,
        grid=(bsz, heads, nb),
        in_specs=[blk(0, False), blk(1, False), blk(3, False), blk(0, True), blk(2, True), blk(3, True),
                  pl.BlockSpec((2, None, 1, HEAD_DIM), lambda b, h, n: (0, h, 0, 0)),
                  st_spec, tri_spec, tri_spec],
        out_specs=[oblk(False), oblk(True), st_spec],
        out_shape=[jax.ShapeDtypeStruct((bsz, length, width), F32),
                   jax.ShapeDtypeStruct((bsz, length, width), F32),
                   jax.ShapeDtypeStruct(s0.shape, F32)],
        scratch_shapes=[pltpu.VMEM((2, HEAD_DIM, HEAD_DIM), F32)],
        compiler_params=_params(("parallel", "parallel", "arbitrary")),
        name="hgrn2",
    )(p, p, p, p, p, p, lb.reshape(2, heads, 1, HEAD_DIM), s0,
      _chunk_tri(lblk, A_CHUNK, False), _chunk_tri(lblk, A_CHUNK, True))


def _even_out_kernel(of_ref, ob_ref, g_ref, u_ref, x_ref, mod_ref, an_ref, band_ref, icnt_ref, pw_ref,
                     ps_ref, wout_ref, nw_ref, o_ref):
    o = of_ref[...] + ob_ref[...]
    g = g_ref[...]
    u = u_ref[...]
    heads = o.shape[1] // HEAD_DIM
    parts = []
    for h in range(heads):
        c = slice(h * HEAD_DIM, (h + 1) * HEAD_DIM)
        parts.append((_rms(o[:, c]) * an_ref[...] * _silu(g[:, c])).astype(BF16))
    gd = u.shape[1] // len(POOL_WINDOWS)
    for gi in range(len(POOL_WINDOWS)):
        c = slice(gi * gd, (gi + 1) * gd)
        ug = u[:, c]
        dlt = _dot(band_ref[gi], ug.astype(BF16)) * icnt_ref[gi] - ug
        parts.append((_dot(dlt.astype(BF16), pw_ref[gi]) * ps_ref[:, c]).astype(BF16))
    y = _dot(jnp.concatenate(parts, axis=1), wout_ref[...])
    o_ref[...] = x_ref[...] + mod_ref[2:3, :] * (_rms(y) * nw_ref[...])


def _pool_consts(tm, roww):
    t = np.arange(tm)
    row, pos = t // roww, t % roww
    bands, icnts = [], []
    for win in POOL_WINDOWS:
        lo = np.clip(pos - win // 2, 0, roww - 1)
        hi = np.clip(pos + win - 1 - win // 2, 0, roww - 1)
        m = (row[:, None] == row[None, :]) & (pos[None, :] >= lo[:, None]) & (pos[None, :] <= hi[:, None])
        bands.append(m)
        icnts.append(np.broadcast_to((1.0 / (hi - lo + 1))[:, None], (tm, 256)))
    return jnp.asarray(np.stack(bands), BF16), jnp.asarray(np.stack(icnts), F32)


def _even_out(o_f, o_b, p, x, mod, layer, row_of_batch, a_norm, pool_w, pool_scale, w_out, nw, roww):
    bsz, length, d = x.shape
    width = o_f.shape[2]
    tm = 256
    assert length % tm == 0 and tm % roww == 0
    band, icnt = _pool_consts(tm, roww)
    ng, gd = pool_w.shape[0], pool_w.shape[1]
    gcol, ucol = p.shape[2] // width - 2, p.shape[2] // width - 1
    tok = lambda w, col: pl.BlockSpec((None, tm, w), lambda b, i: (b, i, col))
    const = lambda shape: pl.BlockSpec(shape, lambda b, i: (0,) * len(shape))
    return pl.pallas_call(
        _even_out_kernel,
        grid=(bsz, length // tm),
        in_specs=[tok(width, 0), tok(width, 0), tok(width, gcol), tok(width, ucol), tok(d, 0),
                  pl.BlockSpec((None, None, 6, d), lambda b, i: (layer, row_of_batch(b), 0, 0)),
                  const((1, HEAD_DIM)), const((ng, tm, tm)), const((ng, tm, 256)), const((ng, gd, gd)),
                  const((1, width)), const((2 * width, d)), const((1, d))],
        out_specs=tok(d, 0),
        out_shape=jax.ShapeDtypeStruct(x.shape, F32),
        compiler_params=_params(("parallel", "parallel")),
        name="even_out",
    )(o_f, o_b, p, p, x, mod, a_norm.reshape(1, HEAD_DIM), band, icnt, pool_w.astype(BF16),
      pool_scale.reshape(1, width), w_out.astype(BF16), nw.reshape(1, d))


def _odd_out_kernel(of_ref, ob_ref, z_ref, x_ref, mod_ref, hn_ref, w_ref, nw_ref, o_ref, acc_scr):
    j = pl.program_id(2)
    o = of_ref[...] + ob_ref[...]
    z = z_ref[...]
    parts = []
    for h in range(o.shape[1] // HEAD_DIM):
        c = slice(h * HEAD_DIM, (h + 1) * HEAD_DIM)
        parts.append((_rms(o[:, c]) * hn_ref[...] * _silu(z[:, c])).astype(BF16))
    y = _dot(jnp.concatenate(parts, axis=1), w_ref[...])

    @pl.when(j == 0)
    def _():
        acc_scr[...] = y

    @pl.when(j > 0)
    def _():
        acc_scr[...] += y

    @pl.when(j == pl.num_programs(2) - 1)
    def _():
        o_ref[...] = x_ref[...] + mod_ref[2:3, :] * (_rms(acc_scr[...]) * nw_ref[...])


def _odd_out(o_f, o_b, p, zcol0, x, mod, layer, row_of_batch, head_norm, w_out, nw, tm=512, tk=1024):
    bsz, length, d = x.shape
    vw = o_f.shape[2]
    tm = min(tm, length)
    zoff = zcol0 // tk
    return pl.pallas_call(
        _odd_out_kernel,
        grid=(bsz, length // tm, vw // tk),
        in_specs=[pl.BlockSpec((None, tm, tk), lambda b, i, j: (b, i, j)),
                  pl.BlockSpec((None, tm, tk), lambda b, i, j: (b, i, j)),
                  pl.BlockSpec((None, tm, tk), lambda b, i, j: (b, i, zoff + j)),
                  pl.BlockSpec((None, tm, d), lambda b, i, j: (b, i, 0)),
                  pl.BlockSpec((None, None, 6, d), lambda b, i, j: (layer, row_of_batch(b), 0, 0)),
                  pl.BlockSpec((1, HEAD_DIM), lambda b, i, j: (0, 0)),
                  pl.BlockSpec((tk, d), lambda b, i, j: (j, 0)),
                  pl.BlockSpec((1, d), lambda b, i, j: (0, 0))],
        out_specs=pl.BlockSpec((None, tm, d), lambda b, i, j: (b, i, 0)),
        out_shape=jax.ShapeDtypeStruct(x.shape, F32),
        scratch_shapes=[pltpu.VMEM((tm, d), F32)],
        compiler_params=_params(("parallel", "parallel", "arbitrary")),
        name="odd_out",
    )(o_f, o_b, p, x, mod, head_norm.reshape(1, HEAD_DIM), w_out.astype(BF16), nw.reshape(1, d))


def _ffn_kernel(x_ref, mod_ref, nwa_ref, nwb_ref, w1_ref, w3_ref, w2_ref, o_ref, h_scr, acc_scr):
    j = pl.program_id(2)

    @pl.when(j == 0)
    def _():
        h = _rms(x_ref[...]) * nwa_ref[...]
        h_scr[...] = (h * (1.0 + mod_ref[4:5, :]) + mod_ref[3:4, :]).astype(BF16)

    h = h_scr[...]
    t = (_silu(_dot(h, w1_ref[...])) * _dot(h, w3_ref[...])).astype(BF16)
    y = _dot(t, w2_ref[...])

    @pl.when(j == 0)
    def _():
        acc_scr[...] = y

    @pl.when(j > 0)
    def _():
        acc_scr[...] += y

    @pl.when(j == pl.num_programs(2) - 1)
    def _():
        o_ref[...] = x_ref[...] + mod_ref[5:6, :] * (_rms(acc_scr[...]) * nwb_ref[...])


def _ffn(x, mod, layer, row_of_batch, nwa, nwb, w13, w2, tm=512, th=512):
    bsz, length, d = x.shape
    hidden = w2.shape[0]
    tm = min(tm, length)
    nh = hidden // th
    return pl.pallas_call(
        _ffn_kernel,
        grid=(bsz, length // tm, nh),
        in_specs=[pl.BlockSpec((None, tm, d), lambda b, i, j: (b, i, 0)),
                  pl.BlockSpec((None, None, 6, d), lambda b, i, j: (layer, row_of_batch(b), 0, 0)),
                  pl.BlockSpec((1, d), lambda b, i, j: (0, 0)),
                  pl.BlockSpec((1, d), lambda b, i, j: (0, 0)),
                  pl.BlockSpec((d, th), lambda b, i, j: (0, j)),
                  pl.BlockSpec((d, th), lambda b, i, j: (0, nh + j)),
                  pl.BlockSpec((th, d), lambda b, i, j: (j, 0))],
        out_specs=pl.BlockSpec((None, tm, d), lambda b, i, j: (b, i, 0)),
        out_shape=jax.ShapeDtypeStruct(x.shape, F32),
        scratch_shapes=[pltpu.VMEM((tm, d), BF16), pltpu.VMEM((tm, d), F32)],
        compiler_params=_params(("parallel", "parallel", "arbitrary")),
        name="ffn",
    )(x, mod, nwa.reshape(1, d), nwb.reshape(1, d), w13, w13, w2)


def _conv_jax(u, w):
    length = u.shape[1]
    left = C_CONV // 2
    up = jnp.pad(u, ((0, 0), (left, C_CONV - 1 - left), (0, 0)))
    out = up[:, 0:length] * w[0]
    for j in range(1, C_CONV):
        out = out + up[:, j:j + length] * w[j]
    return out


def _gdn_scan_jax(q, k, v, g, beta, s0):
    bsz, length, heads, _ = q.shape
    dv = v.shape[-1]
    n = length // C_CHUNK
    rs = lambda t: t.reshape(bsz, n, C_CHUNK, heads, t.shape[-1]).transpose(1, 0, 3, 2, 4)
    q, k, v = rs(q), rs(k), rs(v)
    g = rs(g[..., None])[..., 0]
    beta = rs(beta[..., None])[..., 0]
    gc = jnp.cumsum(g, axis=-1)
    causal = jnp.tril(jnp.ones((C_CHUNK, C_CHUNK), bool))
    strict = jnp.tril(jnp.ones((C_CHUNK, C_CHUNK), bool), -1)
    diff = gc[..., :, None] - gc[..., None, :]
    decay = jnp.where(causal, jnp.exp(jnp.where(causal, diff, 0.0)), 0.0)
    k_beta = k * beta[..., None]
    v_beta = v * beta[..., None]
    a_low = jnp.where(strict, jnp.einsum('nbhtd,nbhsd->nbhts', k_beta, k) * decay, 0.0)
    t_mat = jnp.eye(C_CHUNK, dtype=a_low.dtype) + a_low
    u = lax.linalg.triangular_solve(t_mat, v_beta, left_side=True, lower=True, unit_diagonal=True)
    w = lax.linalg.triangular_solve(t_mat, k_beta * jnp.exp(gc)[..., None], left_side=True,
                                    lower=True, unit_diagonal=True)
    qk = jnp.where(causal, jnp.einsum('nbhtd,nbhsd->nbhts', q, k) * decay, 0.0)
    q_dec = q * jnp.exp(gc)[..., None]
    k_dec = k * jnp.exp(gc[..., -1:] - gc)[..., None]
    g_last = jnp.exp(gc[..., -1])

    def step(state, inp):
        u_i, w_i, qk_i, qd_i, kd_i, gl_i = inp
        v_new = u_i - jnp.einsum('bhtd,bhdv->bhtv', w_i, state)
        o = jnp.einsum('bhtd,bhdv->bhtv', qd_i, state) + jnp.einsum('bhts,bhsv->bhtv', qk_i, v_new)
        state = state * gl_i[..., None, None] + jnp.einsum('bhsd,bhsv->bhdv', kd_i, v_new)
        return state, o

    s_final, o = lax.scan(step, s0, (u, w, qk, q_dec, k_dec, g_last))
    o = o.transpose(1, 0, 3, 2, 4).reshape(bsz, length, heads, dv)
    return o, s_final


def _gdn_jax(p_lat, p_ctx, conv_w, a_log, dt_bias):
    hv = a_log.shape[1]
    kh = hv // 2
    kw, vw = kh * HEAD_DIM, hv * HEAD_DIM
    a_rate = jnp.exp(a_log)
    sh = lambda t: t.reshape(t.shape[:-1] + (t.shape[-1] // HEAD_DIM, HEAD_DIM))
    l2 = lambda t: t * lax.rsqrt(jnp.sum(t * t, axis=-1, keepdims=True) + EPS)

    def project(p):
        qkv = _silu(_conv_jax(p[..., :2 * kw + vw], conv_w))
        q, k, v = jnp.split(qkv, [kw, 2 * kw], axis=-1)
        q = jnp.repeat(l2(sh(q)) * HEAD_DIM ** -0.5, 2, axis=2)
        k = jnp.repeat(l2(sh(k)), 2, axis=2)
        v = sh(v)
        a_f, a_b, b_f, b_b = jnp.split(p[..., 2 * kw + 2 * vw:], 4, axis=-1)
        g_f = -a_rate[0] * jax.nn.softplus(a_f + dt_bias[0])
        g_b = -a_rate[1] * jax.nn.softplus(a_b + dt_bias[1])
        return (q, k, v, g_f, jax.nn.sigmoid(b_f)), (q, k, v, g_b, jax.nn.sigmoid(b_b))

    fwd_l, bwd_l = project(p_lat)
    fwd_c, bwd_c = project(p_ctx)
    bsz = p_lat.shape[0]
    s0 = jnp.zeros((bsz, hv, HEAD_DIM, HEAD_DIM), F32)
    flip = lambda args: tuple(jnp.flip(t, axis=1) for t in args)
    _, sc_f = _gdn_scan_jax(*fwd_c, s0)
    ol_f, _ = _gdn_scan_jax(*fwd_l, sc_f)
    _, sc_b = _gdn_scan_jax(*flip(bwd_c), s0)
    ol_b, _ = _gdn_scan_jax(*flip(bwd_l), sc_b)
    o_f = ol_f.reshape(ol_f.shape[:2] + (vw,))
    o_b = jnp.flip(ol_b, axis=1).reshape(o_f.shape)
    return o_f, o_b


def kernel(x, c, ctx, c_ctx, w_ada, b_ada, norm_w, ev_w_in, ev_lb, ev_a_norm, ev_pool_w, ev_pool_scale,
           ev_w_out, od_w_in, od_conv, od_A_log, od_dt_bias, od_norm, od_w_out, ffn_w13, ffn_w2):
    bsz, seq, d = x.shape
    depth = w_ada.shape[0]
    ctx_len = ctx.shape[1]
    a_width = ev_lb.shape[2]
    heads_a = a_width // HEAD_DIM
    assert bsz + 1 <= 8

    cvec = jnp.zeros((8, d), F32).at[0].set(c_ctx).at[1:1 + bsz].set(c)
    mod = _ada(cvec, w_ada, b_ada).reshape(depth, 8, 6, d)
    lat_row = lambda b: b + 1
    ctx_row = lambda b: 0
    lb_all = jnp.cumsum(jax.nn.softmax(ev_lb.astype(F32), axis=1), axis=1)

    for layer in range(depth):
        need_ctx = layer < depth - 1
        j = layer // 2
        nw = norm_w[layer]
        w13 = ffn_w13[layer].astype(BF16)
        w2 = ffn_w2[layer].astype(BF16)
        if layer % 2 == 0:
            w_in = ev_w_in[j].astype(BF16)
            lb = lb_all[:, layer]
            p_l = _proj(x, mod, layer, lat_row, nw[0], w_in)
            p_c = _proj(ctx, mod, layer, ctx_row, nw[0], w_in)
            s0 = jnp.zeros((2, bsz, heads_a, HEAD_DIM, HEAD_DIM), F32)
            oc_f, oc_b, s_ctx = _hgrn(p_c, lb, s0)
            ol_f, ol_b, _ = _hgrn(p_l, lb, s_ctx)
            x = _even_out(ol_f, ol_b, p_l, x, mod, layer, lat_row, ev_a_norm[j], ev_pool_w[j],
                          ev_pool_scale[j], ev_w_out[j], nw[1], GRID_W)
            if need_ctx:
                ctx = _even_out(oc_f, oc_b, p_c, ctx, mod, layer, ctx_row, ev_a_norm[j], ev_pool_w[j],
                                ev_pool_scale[j], ev_w_out[j], nw[1], ctx_len)
        else:
            hv = od_A_log.shape[2]
            vw = hv * HEAD_DIM
            nmain = 3 * vw
            w_main = od_w_in[j][:, :nmain].astype(BF16)
            wg = _split_bf16(od_w_in[j][:, nmain:])
            p_l, g_l = _proj(x, mod, layer, lat_row, nw[0], w_main, wg)
            p_c, g_c = _proj(ctx, mod, layer, ctx_row, nw[0], w_main, wg)
            o_f, o_b = _gdn_jax(jnp.concatenate([p_l, g_l], -1), jnp.concatenate([p_c, g_c], -1),
                                od_conv[j], od_A_log[j], od_dt_bias[j])
            x = _odd_out(o_f, o_b, p_l, nmain - vw, x, mod, layer, lat_row, od_norm[j], od_w_out[j], nw[1])
            assert not need_ctx
        x = _ffn(x, mod, layer, lat_row, nw[2], nw[3], w13, w2)
        if need_ctx:
            ctx = _ffn(ctx, mod, layer, ctx_row, nw[2], nw[3], w13, w2)
    return x
```

```python
import functools

import numpy as np
import jax
import jax.numpy as jnp
from jax import lax
from jax.experimental import pallas as pl
from jax.experimental.pallas import tpu as pltpu

F32 = jnp.float32
BF16 = jnp.bfloat16
EPS = 1e-6

HEAD_DIM = 128
GRID_W = 64
POOL_WINDOWS = (2, 4, 8, 16)
A_CHUNK = 32
C_CONV = 4
C_CHUNK = 64
DCHUNK = 2 * C_CHUNK
VMEM_LIMIT = 56 * 1024 * 1024

NT = (((1,), (1,)), ((), ()))
TN = (((0,), (0,)), ((), ()))


def _dot(a, b):
    return jnp.dot(a, b, preferred_element_type=F32)


def _dot_nt(a, b):
    return lax.dot_general(a, b, NT, preferred_element_type=F32)


def _dot_tn(a, b):
    return lax.dot_general(a, b, TN, preferred_element_type=F32)


def _silu(x):
    return x * jax.nn.sigmoid(x)


def _rms(x):
    return x * lax.rsqrt(jnp.mean(x * x, axis=-1, keepdims=True) + EPS)


def _split_bf16(x):
    hi = x.astype(BF16)
    lo = (x - hi.astype(F32)).astype(BF16)
    return hi, lo


def _params(sem):
    return pltpu.CompilerParams(dimension_semantics=sem, vmem_limit_bytes=VMEM_LIMIT)


def _ada_kernel(c_ref, w_ref, b_ref, o_ref):
    s = _silu(c_ref[...]).astype(BF16)
    o_ref[...] = _dot(s, w_ref[...].astype(BF16)) + b_ref[...]


def _ada(cvec, w_ada, b_ada):
    depth, d, n = w_ada.shape
    tn = 1024
    return pl.pallas_call(
        _ada_kernel,
        grid=(depth, n // tn),
        in_specs=[
            pl.BlockSpec((8, d), lambda l, j: (0, 0)),
            pl.BlockSpec((None, d, tn), lambda l, j: (l, 0, j)),
            pl.BlockSpec((None, 1, tn), lambda l, j: (l, 0, j)),
        ],
        out_specs=pl.BlockSpec((None, 8, tn), lambda l, j: (l, 0, j)),
        out_shape=jax.ShapeDtypeStruct((depth, 8, n), F32),
        compiler_params=_params(("parallel", "parallel")),
        name="ada",
    )(cvec, w_ada, b_ada.reshape(depth, 1, n))


def _proj_kernel(x_ref, mod_ref, nw_ref, w_ref, *rest, has_gate):
    if has_gate:
        wgh_ref, wgl_ref, o_ref, og_ref, h_scr = rest
    else:
        o_ref, h_scr = rest
    j = pl.program_id(2)

    @pl.when(j == 0)
    def _():
        h = _rms(x_ref[...]) * nw_ref[...]
        h = h * (1.0 + mod_ref[1:2, :]) + mod_ref[0:1, :]
        hh = h.astype(BF16)
        h_scr[...] = hh
        if has_gate:
            hl = (h - hh.astype(F32)).astype(BF16)
            og_ref[...] = _dot(hh, wgh_ref[...]) + _dot(hl, wgh_ref[...]) + _dot(hh, wgl_ref[...])

    o_ref[...] = _dot(h_scr[...], w_ref[...])


def _proj(x, mod, layer, row_of_batch, nw, w, wg=None, tm=512, tn=1024):
    bsz, length, d = x.shape
    n = w.shape[1]
    tm = min(tm, length)
    has_gate = wg is not None
    in_specs = [
        pl.BlockSpec((None, tm, d), lambda b, i, j: (b, i, 0)),
        pl.BlockSpec((None, None, 6, d), lambda b, i, j: (layer, row_of_batch(b), 0, 0)),
        pl.BlockSpec((1, d), lambda b, i, j: (0, 0)),
        pl.BlockSpec((d, tn), lambda b, i, j: (0, j)),
    ]
    out_specs = [pl.BlockSpec((None, tm, tn), lambda b, i, j: (b, i, j))]
    out_shape = [jax.ShapeDtypeStruct((bsz, length, n), F32)]
    args = [x, mod, nw.reshape(1, d), w]
    if has_gate:
        ng = wg[0].shape[1]
        in_specs += [pl.BlockSpec((d, ng), lambda b, i, j: (0, 0))] * 2
        out_specs.append(pl.BlockSpec((None, tm, ng), lambda b, i, j: (b, i, 0)))
        out_shape.append(jax.ShapeDtypeStruct((bsz, length, ng), F32))
        args += list(wg)
    res = pl.pallas_call(
        functools.partial(_proj_kernel, has_gate=has_gate),
        grid=(bsz, length // tm, n // tn),
        in_specs=in_specs,
        out_specs=out_specs,
        out_shape=out_shape,
        scratch_shapes=[pltpu.VMEM((tm, d), BF16)],
        compiler_params=_params(("parallel", "parallel", "arbitrary")),
        name="proj",
    )(*args)
    return res if has_gate else res[0]


def _hgrn_kernel(qf_ref, ff_ref, if_ref, qb_ref, fb_ref, ib_ref, lb_ref, s0_ref, trif_ref, trib_ref,
                 of_ref, ob_ref, sfin_ref, st_scr):
    n = pl.program_id(2)
    lblk = qf_ref.shape[0]
    nchunk = lblk // A_CHUNK

    @pl.when(n == 0)
    def _():
        st_scr[...] = s0_ref[...]

    dirs = ((qf_ref, ff_ref, if_ref, of_ref, trif_ref), (qb_ref, fb_ref, ib_ref, ob_ref, trib_ref))
    for d, (q_ref, f_ref, i_ref, o_ref, tri_ref) in enumerate(dirs):
        lb = lb_ref[d]
        s = f_ref[...]
        logf = jnp.log(lb + (1.0 - lb) * jax.nn.sigmoid(s))
        k = (1.0 - lb) * jax.nn.sigmoid(-s)
        qa = _silu(q_ref[...])
        v = i_ref[...].astype(BF16)
        tri = tri_ref[...]
        hi, lo = _split_bf16(logf)
        b = _dot(tri, hi) + _dot(tri, lo)
        b3 = b.reshape(nchunk, A_CHUNK, HEAD_DIM)
        mid = A_CHUNK // 2 - 1 if d == 0 else A_CHUNK // 2
        last = A_CHUNK - 1 if d == 0 else 0
        bmid = jnp.broadcast_to(b3[:, mid:mid + 1, :], b3.shape).reshape(b.shape)
        blast = jnp.broadcast_to(b3[:, last:last + 1, :], b3.shape).reshape(b.shape)
        qs = (qa * jnp.exp(b - bmid)).astype(BF16)
        ks = (k * jnp.exp(bmid - b)).astype(BF16)
        scores = jnp.where(tri > 0, _dot_nt(qs, ks), 0.0)
        o_intra = _dot(scores.astype(BF16), v)
        q_in = (qa * jnp.exp(b)).astype(BF16)
        k_out = (k * jnp.exp(blast - b)).astype(BF16)
        dec = jnp.exp(blast)
        st = st_scr[d]
        order = range(nchunk) if d == 0 else range(nchunk - 1, -1, -1)
        for c in order:
            r = slice(c * A_CHUNK, (c + 1) * A_CHUNK)
            o_ref[r, :] = o_intra[r] + _dot_nt(q_in[r], st.astype(BF16))
            st = st * dec[c * A_CHUNK:c * A_CHUNK + 1] + _dot_tn(v[r], k_out[r])
        st_scr[d] = st

    @pl.when(n == pl.num_programs(2) - 1)
    def _():
        sfin_ref[...] = st_scr[...]


def _chunk_tri(lblk, chunk, upper):
    t = np.arange(lblk)
    same = (t[:, None] // chunk) == (t[None, :] // chunk)
    tri = (t[None, :] >= t[:, None]) if upper else (t[None, :] <= t[:, None])
    return jnp.asarray(same & tri, BF16)


def _hgrn(p, lb, s0, lblk=256):
    bsz, length, _ = p.shape
    heads = lb.shape[1] // HEAD_DIM
    lblk = min(lblk, length)
    nb = length // lblk
    blk = lambda col, rev: pl.BlockSpec(
        (None, lblk, HEAD_DIM),
        (lambda b, h, n: (b, nb - 1 - n, col * heads + h)) if rev else (lambda b, h, n: (b, n, col * heads + h)))
    oblk = lambda rev: pl.BlockSpec(
        (None, lblk, HEAD_DIM), (lambda b, h, n: (b, nb - 1 - n, h)) if rev else (lambda b, h, n: (b, n, h)))
    st_spec = pl.BlockSpec((2, None, None, HEAD_DIM, HEAD_DIM), lambda b, h, n: (0, b, h, 0, 0))
    tri_spec = pl.BlockSpec((lblk, lblk), lambda b, h, n: (0, 0))
    width = heads * HEAD_DIM
    return pl.pallas_call(
        _hgrn_kernel,
        grid=(bsz, heads, nb),
        in_specs=[blk(0, False), blk(1, False), blk(3, False), blk(0, True), blk(2, True), blk(3, True),
                  pl.BlockSpec((2, None, 1, HEAD_DIM), lambda b, h, n: (0, h, 0, 0)),
                  st_spec, tri_spec, tri_spec],
        out_specs=[oblk(False), oblk(True), st_spec],
        out_shape=[jax.ShapeDtypeStruct((bsz, length, width), F32),
                   jax.ShapeDtypeStruct((bsz, length, width), F32),
                   jax.ShapeDtypeStruct(s0.shape, F32)],
        scratch_shapes=[pltpu.VMEM((2, HEAD_DIM, HEAD_DIM), F32)],
        compiler_params=_params(("parallel", "parallel", "arbitrary")),
        name="hgrn2",
    )(p, p, p, p, p, p, lb.reshape(2, heads, 1, HEAD_DIM), s0,
      _chunk_tri(lblk, A_CHUNK, False), _chunk_tri(lblk, A_CHUNK, True))


def _even_out_kernel(of_ref, ob_ref, g_ref, u_ref, x_ref, mod_ref, an_ref, band_ref, icnt_ref, pw_ref,
                     ps_ref, wout_ref, nw_ref, o_ref):
    o = of_ref[...] + ob_ref[...]
    g = g_ref[...]
    u = u_ref[...]
    heads = o.shape[1] // HEAD_DIM
    parts = []
    for h in range(heads):
        c = slice(h * HEAD_DIM, (h + 1) * HEAD_DIM)
        parts.append((_rms(o[:, c]) * an_ref[...] * _silu(g[:, c])).astype(BF16))
    gd = u.shape[1] // len(POOL_WINDOWS)
    for gi in range(len(POOL_WINDOWS)):
        c = slice(gi * gd, (gi + 1) * gd)
        ug = u[:, c]
        dlt = _dot(band_ref[gi], ug.astype(BF16)) * icnt_ref[gi] - ug
        parts.append((_dot(dlt.astype(BF16), pw_ref[gi]) * ps_ref[:, c]).astype(BF16))
    y = _dot(jnp.concatenate(parts, axis=1), wout_ref[...])
    o_ref[...] = x_ref[...] + mod_ref[2:3, :] * (_rms(y) * nw_ref[...])


def _pool_consts(tm, roww):
    t = np.arange(tm)
    row, pos = t // roww, t % roww
    bands, icnts = [], []
    for win in POOL_WINDOWS:
        lo = np.clip(pos - win // 2, 0, roww - 1)
        hi = np.clip(pos + win - 1 - win // 2, 0, roww - 1)
        m = (row[:, None] == row[None, :]) & (pos[None, :] >= lo[:, None]) & (pos[None, :] <= hi[:, None])
        bands.append(m)
        icnts.append(np.broadcast_to((1.0 / (hi - lo + 1))[:, None], (tm, 256)))
    return jnp.asarray(np.stack(bands), BF16), jnp.asarray(np.stack(icnts), F32)


def _even_out(o_f, o_b, p, x, mod, layer, row_of_batch, a_norm, pool_w, pool_scale, w_out, nw, roww):
    bsz, length, d = x.shape
    width = o_f.shape[2]
    tm = 256
    assert length % tm == 0 and tm % roww == 0
    band, icnt = _pool_consts(tm, roww)
    ng, gd = pool_w.shape[0], pool_w.shape[1]
    gcol, ucol = p.shape[2] // width - 2, p.shape[2] // width - 1
    tok = lambda w, col: pl.BlockSpec((None, tm, w), lambda b, i: (b, i, col))
    const = lambda shape: pl.BlockSpec(shape, lambda b, i: (0,) * len(shape))
    return pl.pallas_call(
        _even_out_kernel,
        grid=(bsz, length // tm),
        in_specs=[tok(width, 0), tok(width, 0), tok(width, gcol), tok(width, ucol), tok(d, 0),
                  pl.BlockSpec((None, None, 6, d), lambda b, i: (layer, row_of_batch(b), 0, 0)),
                  const((1, HEAD_DIM)), const((ng, tm, tm)), const((ng, tm, 256)), const((ng, gd, gd)),
                  const((1, width)), const((2 * width, d)), const((1, d))],
        out_specs=tok(d, 0),
        out_shape=jax.ShapeDtypeStruct(x.shape, F32),
        compiler_params=_params(("parallel", "parallel")),
        name="even_out",
    )(o_f, o_b, p, p, x, mod, a_norm.reshape(1, HEAD_DIM), band, icnt, pool_w.astype(BF16),
      pool_scale.reshape(1, width), w_out.astype(BF16), nw.reshape(1, d))


def _odd_out_kernel(of_ref, ob_ref, z_ref, x_ref, mod_ref, hn_ref, w_ref, nw_ref, o_ref, acc_scr):
    j = pl.program_id(2)
    o = of_ref[...] + ob_ref[...]
    z = z_ref[...]
    parts = []
    for h in range(o.shape[1] // HEAD_DIM):
        c = slice(h * HEAD_DIM, (h + 1) * HEAD_DIM)
        parts.append((_rms(o[:, c]) * hn_ref[...] * _silu(z[:, c])).astype(BF16))
    y = _dot(jnp.concatenate(parts, axis=1), w_ref[...])

    @pl.when(j == 0)
    def _():
        acc_scr[...] = y

    @pl.when(j > 0)
    def _():
        acc_scr[...] += y

    @pl.when(j == pl.num_programs(2) - 1)
    def _():
        o_ref[...] = x_ref[...] + mod_ref[2:3, :] * (_rms(acc_scr[...]) * nw_ref[...])


def _odd_out(o_f, o_b, p, zcol0, x, mod, layer, row_of_batch, head_norm, w_out, nw, tm=512, tk=1024):
    bsz, length, d = x.shape
    vw = o_f.shape[2]
    tm = min(tm, length)
    zoff = zcol0 // tk
    return pl.pallas_call(
        _odd_out_kernel,
        grid=(bsz, length // tm, vw // tk),
        in_specs=[pl.BlockSpec((None, tm, tk), lambda b, i, j: (b, i, j)),
                  pl.BlockSpec((None, tm, tk), lambda b, i, j: (b, i, j)),
                  pl.BlockSpec((None, tm, tk), lambda b, i, j: (b, i, zoff + j)),
                  pl.BlockSpec((None, tm, d), lambda b, i, j: (b, i, 0)),
                  pl.BlockSpec((None, None, 6, d), lambda b, i, j: (layer, row_of_batch(b), 0, 0)),
                  pl.BlockSpec((1, HEAD_DIM), lambda b, i, j: (0, 0)),
                  pl.BlockSpec((tk, d), lambda b, i, j: (j, 0)),
                  pl.BlockSpec((1, d), lambda b, i, j: (0, 0))],
        out_specs=pl.BlockSpec((None, tm, d), lambda b, i, j: (b, i, 0)),
        out_shape=jax.ShapeDtypeStruct(x.shape, F32),
        scratch_shapes=[pltpu.VMEM((tm, d), F32)],
        compiler_params=_params(("parallel", "parallel", "arbitrary")),
        name="odd_out",
    )(o_f, o_b, p, x, mod, head_norm.reshape(1, HEAD_DIM), w_out.astype(BF16), nw.reshape(1, d))


def _ffn_kernel(x_ref, mod_ref, nwa_ref, nwb_ref, w1_ref, w3_ref, w2_ref, o_ref, h_scr, acc_scr):
    j = pl.program_id(2)

    @pl.when(j == 0)
    def _():
        h = _rms(x_ref[...]) * nwa_ref[...]
        h_scr[...] = (h * (1.0 + mod_ref[4:5, :]) + mod_ref[3:4, :]).astype(BF16)

    h = h_scr[...]
    t = (_silu(_dot(h, w1_ref[...])) * _dot(h, w3_ref[...])).astype(BF16)
    y = _dot(t, w2_ref[...])

    @pl.when(j == 0)
    def _():
        acc_scr[...] = y

    @pl.when(j > 0)
    def _():
        acc_scr[...] += y

    @pl.when(j == pl.num_programs(2) - 1)
    def _():
        o_ref[...] = x_ref[...] + mod_ref[5:6, :] * (_rms(acc_scr[...]) * nwb_ref[...])


def _ffn(x, mod, layer, row_of_batch, nwa, nwb, w13, w2, tm=512, th=512):
    bsz, length, d = x.shape
    hidden = w2.shape[0]
    tm = min(tm, length)
    nh = hidden // th
    return pl.pallas_call(
        _ffn_kernel,
        grid=(bsz, length // tm, nh),
        in_specs=[pl.BlockSpec((None, tm, d), lambda b, i, j: (b, i, 0)),
                  pl.BlockSpec((None, None, 6, d), lambda b, i, j: (layer, row_of_batch(b), 0, 0)),
                  pl.BlockSpec((1, d), lambda b, i, j: (0, 0)),
                  pl.BlockSpec((1, d), lambda b, i, j: (0, 0)),
                  pl.BlockSpec((d, th), lambda b, i, j: (0, j)),
                  pl.BlockSpec((d, th), lambda b, i, j: (0, nh + j)),
                  pl.BlockSpec((th, d), lambda b, i, j: (j, 0))],
        out_specs=pl.BlockSpec((None, tm, d), lambda b, i, j: (b, i, 0)),
        out_shape=jax.ShapeDtypeStruct(x.shape, F32),
        scratch_shapes=[pltpu.VMEM((tm, d), BF16), pltpu.VMEM((tm, d), F32)],
        compiler_params=_params(("parallel", "parallel", "arbitrary")),
        name="ffn",
    )(x, mod, nwa.reshape(1, d), nwb.reshape(1, d), w13, w13, w2)


def _gate_kernel(pg_ref, alog_ref, dtb_ref, trif_ref, trib_ref, gcol_ref, grow_ref):
    x = pg_ref[...]
    i8 = lax.broadcasted_iota(jnp.int32, x.shape, 1) & 7
    xa = x + dtb_ref[...]
    softplus = jnp.maximum(xa, 0.0) + jnp.log(1.0 + jnp.exp(-jnp.abs(xa)))
    g = -jnp.exp(alog_ref[...]) * softplus
    g1 = g.astype(BF16)
    r1 = g - g1.astype(F32)
    g2 = r1.astype(BF16)
    g3 = (r1 - g2.astype(F32)).astype(BF16)
    trif, trib = trif_ref[...], trib_ref[...]
    cf = _dot(trif, g1) + _dot(trif, g2) + _dot(trif, g3)
    cb = _dot(trib, g1) + _dot(trib, g2) + _dot(trib, g3)
    out = jnp.where(i8 < 2, cf, jnp.where(i8 < 4, cb, jax.nn.sigmoid(x)))
    gcol_ref[...] = out
    grow_ref[...] = out.T


def _gates(pg, alog_lane, dtb_lane, tm=256):
    bsz, length, nl = pg.shape
    tm = min(tm, length)
    const = lambda shape: pl.BlockSpec(shape, lambda b, i: (0,) * len(shape))
    return pl.pallas_call(
        _gate_kernel,
        grid=(bsz, length // tm),
        in_specs=[pl.BlockSpec((None, tm, nl), lambda b, i: (b, i, 0)), const((1, nl)), const((1, nl)),
                  const((tm, tm)), const((tm, tm))],
        out_specs=[pl.BlockSpec((None, tm, nl), lambda b, i: (b, i, 0)),
                   pl.BlockSpec((None, nl, tm), lambda b, i: (b, 0, i))],
        out_shape=[jax.ShapeDtypeStruct((bsz, length, nl), F32), jax.ShapeDtypeStruct((bsz, nl, length), F32)],
        compiler_params=_params(("parallel", "parallel")),
        name="gdn_gates",
    )(pg, alog_lane, dtb_lane, _chunk_tri(tm, C_CHUNK, False), _chunk_tri(tm, C_CHUNK, True))


def _qkv_kernel(prev_ref, x_ref, next_ref, cw_ref, o_ref, *, l2, scale):
    i = pl.program_id(1)
    tm = x_ref.shape[0]
    prev = jnp.where(i > 0, prev_ref[...], 0.0)
    nxt = jnp.where(i < pl.num_programs(1) - 1, next_ref[...], 0.0)
    xx = jnp.concatenate([prev, x_ref[...], nxt], axis=0)
    acc = None
    for j in range(C_CONV):
        off = 8 - C_CONV // 2 + j
        term = cw_ref[j:j + 1, :] * xx[off:off + tm]
        acc = term if acc is None else acc + term
    y = _silu(acc)
    if l2:
        parts = []
        for h in range(y.shape[1] // HEAD_DIM):
            yh = y[:, h * HEAD_DIM:(h + 1) * HEAD_DIM]
            parts.append(yh * (lax.rsqrt(jnp.sum(yh * yh, axis=-1, keepdims=True) + EPS) * scale))
        y = jnp.concatenate(parts, axis=1)
    o_ref[...] = y.astype(BF16)


def _qkv(p, col0, width, conv_w, l2, scale, tm=256, tc=1024):
    bsz, length, _ = p.shape
    tm = min(tm, length)
    c0 = col0 // tc
    nsub = length // 8
    return pl.pallas_call(
        functools.partial(_qkv_kernel, l2=l2, scale=scale),
        grid=(bsz, length // tm, width // tc),
        in_specs=[pl.BlockSpec((None, 8, tc), lambda b, i, c: (b, jnp.maximum(i * (tm // 8) - 1, 0), c0 + c)),
                  pl.BlockSpec((None, tm, tc), lambda b, i, c: (b, i, c0 + c)),
                  pl.BlockSpec((None, 8, tc),
                               lambda b, i, c: (b, jnp.minimum((i + 1) * (tm // 8), nsub - 1), c0 + c)),
                  pl.BlockSpec((C_CONV, tc), lambda b, i, c: (0, c0 + c))],
        out_specs=pl.BlockSpec((None, tm, tc), lambda b, i, c: (b, i, c)),
        out_shape=jax.ShapeDtypeStruct((bsz, length, width), BF16),
        compiler_params=_params(("parallel", "parallel", "parallel")),
        name="gdn_qkv",
    )(p, p, p, conv_w)


def _gdn_masks():
    t = np.arange(DCHUNK)
    same = (t[:, None] // C_CHUNK) == (t[None, :] // C_CHUNK)
    le, lt = t[None, :] <= t[:, None], t[None, :] < t[:, None]
    masks = [same & le, same & lt, same & le.T, same & lt.T]
    m = 1
    while m < C_CHUNK:
        masks.append(((t[:, None] // (2 * m)) == (t[None, :] // (2 * m))) & ((t[:, None] // m) != (t[None, :] // m)))
        m *= 2
    return jnp.asarray(np.stack(masks), F32)


def _gdn_a_kernel(q_ref, k_ref, gcol_ref, grow_ref, mask_ref, mb_ref, qk_ref, qd_ref, kd_ref, gl_ref):
    j = pl.program_id(1)
    shift = lax.rem(HEAD_DIM - 8 * j, HEAD_DIM)
    nlvl = mask_ref.shape[0] - 5
    rid = lax.broadcasted_iota(jnp.int32, (DCHUNK, DCHUNK), 0)
    cid = lax.broadcasted_iota(jnp.int32, (DCHUNK, DCHUNK), 1)
    eye = (rid == cid).astype(F32)
    first = rid[:, 0:1] < C_CHUNK
    for dc in range(q_ref.shape[0] // DCHUNK):
        rows = slice(dc * DCHUNK, (dc + 1) * DCHUNK)
        q, k = q_ref[rows, :], k_ref[rows, :]
        kk, qk = _dot_nt(k, k), _dot_nt(q, k)
        qf, kf = q.astype(F32), k.astype(F32)
        gc8 = pltpu.roll(gcol_ref[rows, :], shift, 1)
        gr = grow_ref[:, rows]
        gl_rows = []
        for u in range(4):
            d = u // 2
            causal, strict = mask_ref[2 * d], mask_ref[2 * d + 1]
            gcc, bcol = gc8[:, u:u + 1], gc8[:, 4 + u:5 + u]
            gcr, brow = gr[u:u + 1, :], gr[4 + u:5 + u, :]
            dec = jnp.exp(jnp.where(causal > 0, gcc - gcr, 0.0)) * causal
            a = strict * (bcol * kk * dec)
            x = eye - a * mask_ref[4]
            for lvl in range(1, nlvl + 1):
                pm = _dot((a * mask_ref[4 + lvl]).astype(BF16), x.astype(BF16))
                x = x - _dot(x.astype(BF16), pm.astype(BF16))
            mb_ref[u, rows, :] = (x * brow).astype(BF16)
            qk_ref[u, rows, :] = (qk * dec).astype(BF16)
            qd_ref[u, rows, :] = (qf * jnp.exp(gcc)).astype(BF16)
            l0, l1 = (C_CHUNK - 1, DCHUNK - 1) if d == 0 else (0, C_CHUNK)
            gl0, gl1 = gcc[l0:l0 + 1, :], gcc[l1:l1 + 1, :]
            glast = jnp.where(first, gl0, gl1)
            kd_ref[u, rows, :] = (kf * jnp.exp(glast - gcc)).astype(BF16)
            gl_rows += [jnp.broadcast_to(jnp.exp(gl0), (1, HEAD_DIM)),
                        jnp.broadcast_to(jnp.exp(gl1), (1, HEAD_DIM))]
        gl_ref[dc] = jnp.concatenate(gl_rows, axis=0)


def _gdn_a(qn, kn, gcol, grow, lblk=256):
    bsz, length, kw = qn.shape
    kh = kw // HEAD_DIM
    lblk = min(lblk, length)
    masks = _gdn_masks()
    unit = jax.ShapeDtypeStruct((bsz, kh, 4, length, HEAD_DIM), BF16)
    unit_spec = pl.BlockSpec((None, None, 4, lblk, HEAD_DIM), lambda b, j, n: (b, j, 0, n, 0))
    return pl.pallas_call(
        _gdn_a_kernel,
        grid=(bsz, kh, length // lblk),
        in_specs=[pl.BlockSpec((None, lblk, HEAD_DIM), lambda b, j, n: (b, n, j)),
                  pl.BlockSpec((None, lblk, HEAD_DIM), lambda b, j, n: (b, n, j)),
                  pl.BlockSpec((None, lblk, HEAD_DIM), lambda b, j, n: (b, n, 0)),
                  pl.BlockSpec((None, 8, lblk), lambda b, j, n: (b, j, n)),
                  pl.BlockSpec(masks.shape, lambda b, j, n: (0, 0, 0))],
        out_specs=[unit_spec, unit_spec, unit_spec, unit_spec,
                   pl.BlockSpec((None, None, lblk // DCHUNK, 8, HEAD_DIM), lambda b, j, n: (b, j, n, 0, 0))],
        out_shape=[unit, unit, unit, unit,
                   jax.ShapeDtypeStruct((bsz, kh, length // DCHUNK, 8, HEAD_DIM), F32)],
        compiler_params=_params(("parallel", "parallel", "parallel")),
        name="gdn_local",
    )(qn, kn, gcol, grow, masks)


def _gdn_b_kernel(*refs):
    s0_ref, of_ref, ob_ref, sfin_ref, st_scr = refs[16:]
    n = pl.program_id(2)

    @pl.when(n == 0)
    def _():
        st_scr[...] = s0_ref[...]

    zeros = jnp.zeros((C_CHUNK, HEAD_DIM), BF16)
    for d, o_ref in enumerate((of_ref, ob_ref)):
        k_ref, v_ref, mb_ref, qk_ref, qd_ref, kd_ref, gl_ref, gr_ref = refs[8 * d:8 * d + 8]
        ndc = k_ref.shape[0] // DCHUNK
        chunks = [(dc, cc) for dc in range(ndc) for cc in range(2)]
        if d == 1:
            chunks = chunks[::-1]
        for e in range(2):
            u = 2 * d + e
            cols = slice(e * HEAD_DIM, (e + 1) * HEAD_DIM)
            s = st_scr[u]
            for dc, cc in chunks:
                big = slice(dc * DCHUNK, (dc + 1) * DCHUNK)
                r = slice(dc * DCHUNK + cc * C_CHUNK, dc * DCHUNK + (cc + 1) * C_CHUNK)
                place = (lambda t: jnp.concatenate([t, zeros], axis=0)) if cc == 0 else (
                    lambda t: jnp.concatenate([zeros, t], axis=0))
                sb = s.astype(BF16)
                ks = _dot(k_ref[r, :], sb).astype(BF16)
                mb = mb_ref[e, r, :]
                mw = (mb.astype(F32) * jnp.exp(gr_ref[u:u + 1, big])).astype(BF16)
                vn = (_dot(mb, v_ref[big, cols]) - _dot(mw, place(ks))).astype(BF16)
                o_ref[r, cols] = _dot(qd_ref[e, r, :], sb) + _dot(qk_ref[e, r, :], place(vn))
                s = s * gl_ref[dc, 2 * u + cc:2 * u + cc + 1, :] + _dot_tn(kd_ref[e, r, :], vn)
            st_scr[u] = s

    @pl.when(n == pl.num_programs(2) - 1)
    def _():
        sfin_ref[...] = st_scr[...]


def _gdn_b(kn, vv, mb, qk, qd, kd, gl, grow, s0, lblk=256):
    bsz, length, kw = kn.shape
    kh = kw // HEAD_DIM
    lblk = min(lblk, length)
    nb = length // lblk
    ndc = lblk // DCHUNK

    def specs(d):
        blk = (lambda n: nb - 1 - n) if d else (lambda n: n)
        unit = pl.BlockSpec((None, None, 2, lblk, HEAD_DIM), lambda b, j, n: (b, j, d, blk(n), 0))
        return [pl.BlockSpec((None, lblk, HEAD_DIM), lambda b, j, n: (b, blk(n), j)),
                pl.BlockSpec((None, lblk, 2 * HEAD_DIM), lambda b, j, n: (b, blk(n), j)),
                unit, unit, unit, unit,
                pl.BlockSpec((None, None, ndc, 8, HEAD_DIM), lambda b, j, n: (b, j, blk(n), 0, 0)),
                pl.BlockSpec((None, 8, lblk), lambda b, j, n: (b, j, blk(n)))]

    st_spec = pl.BlockSpec((None, None, 4, HEAD_DIM, HEAD_DIM), lambda b, j, n: (b, j, 0, 0, 0))
    o_shape = jax.ShapeDtypeStruct((bsz, length, 2 * kw), F32)
    args = [kn, vv, mb, qk, qd, kd, gl, grow]
    return pl.pallas_call(
        _gdn_b_kernel,
        grid=(bsz, kh, nb),
        in_specs=specs(0) + specs(1) + [st_spec],
        out_specs=[pl.BlockSpec((None, lblk, 2 * HEAD_DIM), lambda b, j, n: (b, n, j)),
                   pl.BlockSpec((None, lblk, 2 * HEAD_DIM), lambda b, j, n: (b, nb - 1 - n, j)),
                   st_spec],
        out_shape=[o_shape, o_shape, jax.ShapeDtypeStruct(s0.shape, F32)],
        scratch_shapes=[pltpu.VMEM((4, HEAD_DIM, HEAD_DIM), F32)],
        compiler_params=_params(("parallel", "parallel", "arbitrary")),
        name="gdn_scan",
    )(*args, *args, s0)


def _gate_layout(hv):
    perm, src = [], []
    for j in range(hv // 2):
        for i in range(8):
            u = i % 4
            d, e = u // 2, u % 2
            perm.append((0 if i < 4 else 2 * hv) + d * hv + 2 * j + e)
            src.append(d * hv + 2 * j + e if i < 4 else -1)
    return np.asarray(perm), np.asarray(src)


def kernel(x, c, ctx, c_ctx, w_ada, b_ada, norm_w, ev_w_in, ev_lb, ev_a_norm, ev_pool_w, ev_pool_scale,
           ev_w_out, od_w_in, od_conv, od_A_log, od_dt_bias, od_norm, od_w_out, ffn_w13, ffn_w2):
    bsz, seq, d = x.shape
    depth = w_ada.shape[0]
    ctx_len = ctx.shape[1]
    a_width = ev_lb.shape[2]
    heads_a = a_width // HEAD_DIM
    assert bsz + 1 <= 8

    cvec = jnp.zeros((8, d), F32).at[0].set(c_ctx).at[1:1 + bsz].set(c)
    mod = _ada(cvec, w_ada, b_ada).reshape(depth, 8, 6, d)
    lat_row = lambda b: b + 1
    ctx_row = lambda b: 0
    lb_all = jnp.cumsum(jax.nn.softmax(ev_lb.astype(F32), axis=1), axis=1)

    for layer in range(depth):
        need_ctx = layer < depth - 1
        j = layer // 2
        nw = norm_w[layer]
        w13 = ffn_w13[layer].astype(BF16)
        w2 = ffn_w2[layer].astype(BF16)
        if layer % 2 == 0:
            w_in = ev_w_in[j].astype(BF16)
            lb = lb_all[:, layer]
            p_l = _proj(x, mod, layer, lat_row, nw[0], w_in)
            p_c = _proj(ctx, mod, layer, ctx_row, nw[0], w_in)
            s0 = jnp.zeros((2, bsz, heads_a, HEAD_DIM, HEAD_DIM), F32)
            oc_f, oc_b, s_ctx = _hgrn(p_c, lb, s0)
            ol_f, ol_b, _ = _hgrn(p_l, lb, s_ctx)
            x = _even_out(ol_f, ol_b, p_l, x, mod, layer, lat_row, ev_a_norm[j], ev_pool_w[j],
                          ev_pool_scale[j], ev_w_out[j], nw[1], GRID_W)
            if need_ctx:
                ctx = _even_out(oc_f, oc_b, p_c, ctx, mod, layer, ctx_row, ev_a_norm[j], ev_pool_w[j],
                                ev_pool_scale[j], ev_w_out[j], nw[1], ctx_len)
        else:
            assert not need_ctx
            hv = od_A_log.shape[2]
            kh = hv // 2
            kw, vw = kh * HEAD_DIM, hv * HEAD_DIM
            nmain = 2 * kw + 2 * vw
            w_main = od_w_in[j][:, :nmain].astype(BF16)
            perm, src = _gate_layout(hv)
            wg = _split_bf16(od_w_in[j][:, nmain:][:, perm])
            lane_param = lambda t: jnp.where(src >= 0, t.reshape(-1)[np.maximum(src, 0)], 0.0).reshape(1, -1)
            alog_lane, dtb_lane = lane_param(od_A_log[j]), lane_param(od_dt_bias[j])
            conv_w = od_conv[j]

            def mix(p, pg, s0):
                gcol, grow = _gates(pg, alog_lane, dtb_lane)
                qn = _qkv(p, 0, kw, conv_w, True, HEAD_DIM ** -0.5)
                kn = _qkv(p, kw, kw, conv_w, True, 1.0)
                vv = _qkv(p, 2 * kw, vw, conv_w, False, 1.0)
                mb, qk, qd, kd, gl = _gdn_a(qn, kn, gcol, grow)
                return _gdn_b(kn, vv, mb, qk, qd, kd, gl, grow, s0)

            p_l, g_l = _proj(x, mod, layer, lat_row, nw[0], w_main, wg)
            p_c, g_c = _proj(ctx, mod, layer, ctx_row, nw[0], w_main, wg)
            s0 = jnp.zeros((bsz, kh, 4, HEAD_DIM, HEAD_DIM), F32)
            _, _, s_ctx = mix(p_c, g_c, s0)
            o_f, o_b, _ = mix(p_l, g_l, s_ctx)
            x = _odd_out(o_f, o_b, p_l, 2 * kw + vw, x, mod, layer, lat_row, od_norm[j], od_w_out[j], nw[1])
        x = _ffn(x, mod, layer, lat_row, nw[2], nw[3], w13, w2)
        if need_ctx:
            ctx = _ffn(ctx, mod, layer, ctx_row, nw[2], nw[3], w13, w2)
    return x
```

```python
import functools

import numpy as np
import jax
import jax.numpy as jnp
from jax import lax
from jax.experimental import pallas as pl
from jax.experimental.pallas import tpu as pltpu

F32 = jnp.float32
BF16 = jnp.bfloat16
EPS = 1e-6

HEAD_DIM = 128
GRID_W = 64
POOL_WINDOWS = (2, 4, 8, 16)
A_CHUNK = 32
C_CONV = 4
C_CHUNK = 64
DCHUNK = 2 * C_CHUNK
VMEM_LIMIT = 56 * 1024 * 1024

NT = (((1,), (1,)), ((), ()))
TN = (((0,), (0,)), ((), ()))


def _dot(a, b):
    return jnp.dot(a, b, preferred_element_type=F32)


def _dot_nt(a, b):
    return lax.dot_general(a, b, NT, preferred_element_type=F32)


def _dot_tn(a, b):
    return lax.dot_general(a, b, TN, preferred_element_type=F32)


def _silu(x):
    return x * jax.nn.sigmoid(x)


def _rms(x):
    return x * lax.rsqrt(jnp.mean(x * x, axis=-1, keepdims=True) + EPS)


def _split_bf16(x):
    hi = x.astype(BF16)
    lo = (x - hi.astype(F32)).astype(BF16)
    return hi, lo


def _params(sem):
    return pltpu.CompilerParams(dimension_semantics=sem, vmem_limit_bytes=VMEM_LIMIT)


def _ada_kernel(c_ref, w_ref, b_ref, o_ref):
    s = _silu(c_ref[...]).astype(BF16)
    o_ref[...] = _dot(s, w_ref[...].astype(BF16)) + b_ref[...]


def _ada(cvec, w_ada, b_ada):
    depth, d, n = w_ada.shape
    tn = 1024
    return pl.pallas_call(
        _ada_kernel,
        grid=(depth, n // tn),
        in_specs=[
            pl.BlockSpec((8, d), lambda l, j: (0, 0)),
            pl.BlockSpec((None, d, tn), lambda l, j: (l, 0, j)),
            pl.BlockSpec((None, 1, tn), lambda l, j: (l, 0, j)),
        ],
        out_specs=pl.BlockSpec((None, 8, tn), lambda l, j: (l, 0, j)),
        out_shape=jax.ShapeDtypeStruct((depth, 8, n), F32),
        compiler_params=_params(("parallel", "parallel")),
        name="ada",
    )(cvec, w_ada, b_ada.reshape(depth, 1, n))


def _proj_kernel(x_ref, mod_ref, nw_ref, w_ref, *rest, has_gate):
    if has_gate:
        wgh_ref, wgl_ref, o_ref, og_ref, h_scr = rest
    else:
        o_ref, h_scr = rest
    j = pl.program_id(2)

    @pl.when(j == 0)
    def _():
        h = _rms(x_ref[...]) * nw_ref[...]
        h = h * (1.0 + mod_ref[1:2, :]) + mod_ref[0:1, :]
        hh = h.astype(BF16)
        h_scr[...] = hh
        if has_gate:
            hl = (h - hh.astype(F32)).astype(BF16)
            og_ref[...] = _dot(hh, wgh_ref[...]) + _dot(hl, wgh_ref[...]) + _dot(hh, wgl_ref[...])

    o_ref[...] = _dot(h_scr[...], w_ref[...])


def _proj(x, mod, layer, row_of_batch, nw, w, wg=None, tm=512, tn=1024):
    bsz, length, d = x.shape
    n = w.shape[1]
    tm = min(tm, length)
    has_gate = wg is not None
    in_specs = [
        pl.BlockSpec((None, tm, d), lambda b, i, j: (b, i, 0)),
        pl.BlockSpec((None, None, 6, d), lambda b, i, j: (layer, row_of_batch(b), 0, 0)),
        pl.BlockSpec((1, d), lambda b, i, j: (0, 0)),
        pl.BlockSpec((d, tn), lambda b, i, j: (0, j)),
    ]
    out_specs = [pl.BlockSpec((None, tm, tn), lambda b, i, j: (b, i, j))]
    out_shape = [jax.ShapeDtypeStruct((bsz, length, n), F32)]
    args = [x, mod, nw.reshape(1, d), w]
    if has_gate:
        ng = wg[0].shape[1]
        in_specs += [pl.BlockSpec((d, ng), lambda b, i, j: (0, 0))] * 2
        out_specs.append(pl.BlockSpec((None, tm, ng), lambda b, i, j: (b, i, 0)))
        out_shape.append(jax.ShapeDtypeStruct((bsz, length, ng), F32))
        args += list(wg)
    res = pl.pallas_call(
        functools.partial(_proj_kernel, has_gate=has_gate),
        grid=(bsz, length // tm, n // tn),
        in_specs=in_specs,
        out_specs=out_specs,
        out_shape=out_shape,
        scratch_shapes=[pltpu.VMEM((tm, d), BF16)],
        compiler_params=_params(("parallel", "parallel", "arbitrary")),
        name="proj",
    )(*args)
    return res if has_gate else res[0]


def _hgrn_kernel(qf_ref, ff_ref, if_ref, qb_ref, fb_ref, ib_ref, lb_ref, s0_ref, trif_ref, trib_ref,
                 of_ref, ob_ref, sfin_ref, st_scr):
    n = pl.program_id(2)
    lblk = qf_ref.shape[0]
    nchunk = lblk // A_CHUNK

    @pl.when(n == 0)
    def _():
        st_scr[...] = s0_ref[...]

    dirs = ((qf_ref, ff_ref, if_ref, of_ref, trif_ref), (qb_ref, fb_ref, ib_ref, ob_ref, trib_ref))
    for d, (q_ref, f_ref, i_ref, o_ref, tri_ref) in enumerate(dirs):
        lb = lb_ref[d]
        s = f_ref[...]
        logf = jnp.log(lb + (1.0 - lb) * jax.nn.sigmoid(s))
        k = (1.0 - lb) * jax.nn.sigmoid(-s)
        qa = _silu(q_ref[...])
        v = i_ref[...].astype(BF16)
        tri = tri_ref[...]
        hi, lo = _split_bf16(logf)
        b = _dot(tri, hi) + _dot(tri, lo)
        b3 = b.reshape(nchunk, A_CHUNK, HEAD_DIM)
        mid = A_CHUNK // 2 - 1 if d == 0 else A_CHUNK // 2
        last = A_CHUNK - 1 if d == 0 else 0
        bmid = jnp.broadcast_to(b3[:, mid:mid + 1, :], b3.shape).reshape(b.shape)
        blast = jnp.broadcast_to(b3[:, last:last + 1, :], b3.shape).reshape(b.shape)
        qs = (qa * jnp.exp(b - bmid)).astype(BF16)
        ks = (k * jnp.exp(bmid - b)).astype(BF16)
        scores = jnp.where(tri > 0, _dot_nt(qs, ks), 0.0)
        o_intra = _dot(scores.astype(BF16), v)
        q_in = (qa * jnp.exp(b)).astype(BF16)
        k_out = (k * jnp.exp(blast - b)).astype(BF16)
        dec = jnp.exp(blast)
        st = st_scr[d]
        order = range(nchunk) if d == 0 else range(nchunk - 1, -1, -1)
        for c in order:
            r = slice(c * A_CHUNK, (c + 1) * A_CHUNK)
            o_ref[r, :] = o_intra[r] + _dot_nt(q_in[r], st.astype(BF16))
            st = st * dec[c * A_CHUNK:c * A_CHUNK + 1] + _dot_tn(v[r], k_out[r])
        st_scr[d] = st

    @pl.when(n == pl.num_programs(2) - 1)
    def _():
        sfin_ref[...] = st_scr[...]


def _chunk_tri(lblk, chunk, upper):
    t = np.arange(lblk)
    same = (t[:, None] // chunk) == (t[None, :] // chunk)
    tri = (t[None, :] >= t[:, None]) if upper else (t[None, :] <= t[:, None])
    return jnp.asarray(same & tri, BF16)


def _hgrn(p, lb, s0, lblk=256):
    bsz, length, _ = p.shape
    heads = lb.shape[1] // HEAD_DIM
    lblk = min(lblk, length)
    nb = length // lblk
    blk = lambda col, rev: pl.BlockSpec(
        (None, lblk, HEAD_DIM),
        (lambda b, h, n: (b, nb - 1 - n, col * heads + h)) if rev else (lambda b, h, n: (b, n, col * heads + h)))
    oblk = lambda rev: pl.BlockSpec(
        (None, lblk, HEAD_DIM), (lambda b, h, n: (b, nb - 1 - n, h)) if rev else (lambda b, h, n: (b, n, h)))
    st_spec = pl.BlockSpec((2, None, None, HEAD_DIM, HEAD_DIM), lambda b, h, n: (0, b, h, 0, 0))
    tri_spec = pl.BlockSpec((lblk, lblk), lambda b, h, n: (0, 0))
    width = heads * HEAD_DIM
    return pl.pallas_call(
        _hgrn_kernel,
        grid=(bsz, heads, nb),
        in_specs=[blk(0, False), blk(1, False), blk(3, False), blk(0, True), blk(2, True), blk(3, True),
                  pl.BlockSpec((2, None, 1, HEAD_DIM), lambda b, h, n: (0, h, 0, 0)),
                  st_spec, tri_spec, tri_spec],
        out_specs=[oblk(False), oblk(True), st_spec],
        out_shape=[jax.ShapeDtypeStruct((bsz, length, width), F32),
                   jax.ShapeDtypeStruct((bsz, length, width), F32),
                   jax.ShapeDtypeStruct(s0.shape, F32)],
        scratch_shapes=[pltpu.VMEM((2, HEAD_DIM, HEAD_DIM), F32)],
        compiler_params=_params(("parallel", "parallel", "arbitrary")),
        name="hgrn2",
    )(p, p, p, p, p, p, lb.reshape(2, heads, 1, HEAD_DIM), s0,
      _chunk_tri(lblk, A_CHUNK, False), _chunk_tri(lblk, A_CHUNK, True))


def _even_out_kernel(of_ref, ob_ref, g_ref, u_ref, x_ref, mod_ref, an_ref, band_ref, icnt_ref, pw_ref,
                     ps_ref, wout_ref, nw_ref, o_ref):
    o = of_ref[...] + ob_ref[...]
    g = g_ref[...]
    u = u_ref[...]
    heads = o.shape[1] // HEAD_DIM
    parts = []
    for h in range(heads):
        c = slice(h * HEAD_DIM, (h + 1) * HEAD_DIM)
        parts.append((_rms(o[:, c]) * an_ref[...] * _silu(g[:, c])).astype(BF16))
    gd = u.shape[1] // len(POOL_WINDOWS)
    for gi in range(len(POOL_WINDOWS)):
        c = slice(gi * gd, (gi + 1) * gd)
        ug = u[:, c]
        dlt = _dot(band_ref[gi], ug.astype(BF16)) * icnt_ref[gi] - ug
        parts.append((_dot(dlt.astype(BF16), pw_ref[gi]) * ps_ref[:, c]).astype(BF16))
    y = _dot(jnp.concatenate(parts, axis=1), wout_ref[...])
    o_ref[...] = x_ref[...] + mod_ref[2:3, :] * (_rms(y) * nw_ref[...])


def _pool_consts(tm, roww):
    t = np.arange(tm)
    row, pos = t // roww, t % roww
    bands, icnts = [], []
    for win in POOL_WINDOWS:
        lo = np.clip(pos - win // 2, 0, roww - 1)
        hi = np.clip(pos + win - 1 - win // 2, 0, roww - 1)
        m = (row[:, None] == row[None, :]) & (pos[None, :] >= lo[:, None]) & (pos[None, :] <= hi[:, None])
        bands.append(m)
        icnts.append(np.broadcast_to((1.0 / (hi - lo + 1))[:, None], (tm, 256)))
    return jnp.asarray(np.stack(bands), BF16), jnp.asarray(np.stack(icnts), F32)


def _even_out(o_f, o_b, p, x, mod, layer, row_of_batch, a_norm, pool_w, pool_scale, w_out, nw, roww):
    bsz, length, d = x.shape
    width = o_f.shape[2]
    tm = 256
    assert length % tm == 0 and tm % roww == 0
    band, icnt = _pool_consts(tm, roww)
    ng, gd = pool_w.shape[0], pool_w.shape[1]
    gcol, ucol = p.shape[2] // width - 2, p.shape[2] // width - 1
    tok = lambda w, col: pl.BlockSpec((None, tm, w), lambda b, i: (b, i, col))
    const = lambda shape: pl.BlockSpec(shape, lambda b, i: (0,) * len(shape))
    return pl.pallas_call(
        _even_out_kernel,
        grid=(bsz, length // tm),
        in_specs=[tok(width, 0), tok(width, 0), tok(width, gcol), tok(width, ucol), tok(d, 0),
                  pl.BlockSpec((None, None, 6, d), lambda b, i: (layer, row_of_batch(b), 0, 0)),
                  const((1, HEAD_DIM)), const((ng, tm, tm)), const((ng, tm, 256)), const((ng, gd, gd)),
                  const((1, width)), const((2 * width, d)), const((1, d))],
        out_specs=tok(d, 0),
        out_shape=jax.ShapeDtypeStruct(x.shape, F32),
        compiler_params=_params(("parallel", "parallel")),
        name="even_out",
    )(o_f, o_b, p, p, x, mod, a_norm.reshape(1, HEAD_DIM), band, icnt, pool_w.astype(BF16),
      pool_scale.reshape(1, width), w_out.astype(BF16), nw.reshape(1, d))


def _odd_out_kernel(of_ref, ob_ref, z_ref, x_ref, mod_ref, hn_ref, w_ref, nw_ref, o_ref, acc_scr):
    j = pl.program_id(2)
    o = of_ref[...] + ob_ref[...]
    z = z_ref[...]
    parts = []
    for h in range(o.shape[1] // HEAD_DIM):
        c = slice(h * HEAD_DIM, (h + 1) * HEAD_DIM)
        parts.append((_rms(o[:, c]) * hn_ref[...] * _silu(z[:, c])).astype(BF16))
    y = _dot(jnp.concatenate(parts, axis=1), w_ref[...])

    @pl.when(j == 0)
    def _():
        acc_scr[...] = y

    @pl.when(j > 0)
    def _():
        acc_scr[...] += y

    @pl.when(j == pl.num_programs(2) - 1)
    def _():
        o_ref[...] = x_ref[...] + mod_ref[2:3, :] * (_rms(acc_scr[...]) * nw_ref[...])


def _odd_out(o_f, o_b, p, zcol0, x, mod, layer, row_of_batch, head_norm, w_out, nw, tm=512, tk=1024):
    bsz, length, d = x.shape
    vw = o_f.shape[2]
    tm = min(tm, length)
    zoff = zcol0 // tk
    return pl.pallas_call(
        _odd_out_kernel,
        grid=(bsz, length // tm, vw // tk),
        in_specs=[pl.BlockSpec((None, tm, tk), lambda b, i, j: (b, i, j)),
                  pl.BlockSpec((None, tm, tk), lambda b, i, j: (b, i, j)),
                  pl.BlockSpec((None, tm, tk), lambda b, i, j: (b, i, zoff + j)),
                  pl.BlockSpec((None, tm, d), lambda b, i, j: (b, i, 0)),
                  pl.BlockSpec((None, None, 6, d), lambda b, i, j: (layer, row_of_batch(b), 0, 0)),
                  pl.BlockSpec((1, HEAD_DIM), lambda b, i, j: (0, 0)),
                  pl.BlockSpec((tk, d), lambda b, i, j: (j, 0)),
                  pl.BlockSpec((1, d), lambda b, i, j: (0, 0))],
        out_specs=pl.BlockSpec((None, tm, d), lambda b, i, j: (b, i, 0)),
        out_shape=jax.ShapeDtypeStruct(x.shape, F32),
        scratch_shapes=[pltpu.VMEM((tm, d), F32)],
        compiler_params=_params(("parallel", "parallel", "arbitrary")),
        name="odd_out",
    )(o_f, o_b, p, x, mod, head_norm.reshape(1, HEAD_DIM), w_out.astype(BF16), nw.reshape(1, d))


def _ffn_kernel(x_ref, mod_ref, nwa_ref, nwb_ref, w1_ref, w3_ref, w2_ref, o_ref, h_scr, acc_scr):
    j = pl.program_id(2)

    @pl.when(j == 0)
    def _():
        h = _rms(x_ref[...]) * nwa_ref[...]
        h_scr[...] = (h * (1.0 + mod_ref[4:5, :]) + mod_ref[3:4, :]).astype(BF16)

    h = h_scr[...]
    t = (_silu(_dot(h, w1_ref[...])) * _dot(h, w3_ref[...])).astype(BF16)
    y = _dot(t, w2_ref[...])

    @pl.when(j == 0)
    def _():
        acc_scr[...] = y

    @pl.when(j > 0)
    def _():
        acc_scr[...] += y

    @pl.when(j == pl.num_programs(2) - 1)
    def _():
        o_ref[...] = x_ref[...] + mod_ref[5:6, :] * (_rms(acc_scr[...]) * nwb_ref[...])


def _ffn(x, mod, layer, row_of_batch, nwa, nwb, w13, w2, tm=512, th=512):
    bsz, length, d = x.shape
    hidden = w2.shape[0]
    tm = min(tm, length)
    nh = hidden // th
    return pl.pallas_call(
        _ffn_kernel,
        grid=(bsz, length // tm, nh),
        in_specs=[pl.BlockSpec((None, tm, d), lambda b, i, j: (b, i, 0)),
                  pl.BlockSpec((None, None, 6, d), lambda b, i, j: (layer, row_of_batch(b), 0, 0)),
                  pl.BlockSpec((1, d), lambda b, i, j: (0, 0)),
                  pl.BlockSpec((1, d), lambda b, i, j: (0, 0)),
                  pl.BlockSpec((d, th), lambda b, i, j: (0, j)),
                  pl.BlockSpec((d, th), lambda b, i, j: (0, nh + j)),
                  pl.BlockSpec((th, d), lambda b, i, j: (j, 0))],
        out_specs=pl.BlockSpec((None, tm, d), lambda b, i, j: (b, i, 0)),
        out_shape=jax.ShapeDtypeStruct(x.shape, F32),
        scratch_shapes=[pltpu.VMEM((tm, d), BF16), pltpu.VMEM((tm, d), F32)],
        compiler_params=_params(("parallel", "parallel", "arbitrary")),
        name="ffn",
    )(x, mod, nwa.reshape(1, d), nwb.reshape(1, d), w13, w13, w2)


def _gate_kernel(pg_ref, alog_ref, dtb_ref, trif_ref, trib_ref, gcol_ref, grow_ref):
    x = pg_ref[...]
    i8 = lax.broadcasted_iota(jnp.int32, x.shape, 1) & 7
    xa = x + dtb_ref[...]
    softplus = jnp.maximum(xa, 0.0) + jnp.log(1.0 + jnp.exp(-jnp.abs(xa)))
    g = -jnp.exp(alog_ref[...]) * softplus
    g1 = g.astype(BF16)
    r1 = g - g1.astype(F32)
    g2 = r1.astype(BF16)
    g3 = (r1 - g2.astype(F32)).astype(BF16)
    trif, trib = trif_ref[...], trib_ref[...]
    cf = _dot(trif, g1) + _dot(trif, g2) + _dot(trif, g3)
    cb = _dot(trib, g1) + _dot(trib, g2) + _dot(trib, g3)
    out = jnp.where(i8 < 2, cf, jnp.where(i8 < 4, cb, jax.nn.sigmoid(x)))
    gcol_ref[...] = out
    grow_ref[...] = out.T


def _gates(pg, alog_lane, dtb_lane, tm=256):
    bsz, length, nl = pg.shape
    tm = min(tm, length)
    const = lambda shape: pl.BlockSpec(shape, lambda b, i: (0,) * len(shape))
    return pl.pallas_call(
        _gate_kernel,
        grid=(bsz, length // tm),
        in_specs=[pl.BlockSpec((None, tm, nl), lambda b, i: (b, i, 0)), const((1, nl)), const((1, nl)),
                  const((tm, tm)), const((tm, tm))],
        out_specs=[pl.BlockSpec((None, tm, nl), lambda b, i: (b, i, 0)),
                   pl.BlockSpec((None, nl, tm), lambda b, i: (b, 0, i))],
        out_shape=[jax.ShapeDtypeStruct((bsz, length, nl), F32), jax.ShapeDtypeStruct((bsz, nl, length), F32)],
        compiler_params=_params(("parallel", "parallel")),
        name="gdn_gates",
    )(pg, alog_lane, dtb_lane, _chunk_tri(tm, C_CHUNK, False), _chunk_tri(tm, C_CHUNK, True))


def _qkv_kernel(prev_ref, x_ref, next_ref, cw_ref, o_ref, *, l2, scale):
    i = pl.program_id(1)
    tm = x_ref.shape[0]
    prev = jnp.where(i > 0, prev_ref[...], 0.0)
    nxt = jnp.where(i < pl.num_programs(1) - 1, next_ref[...], 0.0)
    xx = jnp.concatenate([prev, x_ref[...], nxt], axis=0)
    acc = None
    for j in range(C_CONV):
        off = 8 - C_CONV // 2 + j
        term = cw_ref[j:j + 1, :] * xx[off:off + tm]
        acc = term if acc is None else acc + term
    y = _silu(acc)
    if l2:
        parts = []
        for h in range(y.shape[1] // HEAD_DIM):
            yh = y[:, h * HEAD_DIM:(h + 1) * HEAD_DIM]
            parts.append(yh * (lax.rsqrt(jnp.sum(yh * yh, axis=-1, keepdims=True) + EPS) * scale))
        y = jnp.concatenate(parts, axis=1)
    o_ref[...] = y.astype(BF16)


def _qkv(p, col0, width, conv_w, l2, scale, tm=256, tc=1024):
    bsz, length, _ = p.shape
    tm = min(tm, length)
    c0 = col0 // tc
    nsub = length // 8
    return pl.pallas_call(
        functools.partial(_qkv_kernel, l2=l2, scale=scale),
        grid=(bsz, length // tm, width // tc),
        in_specs=[pl.BlockSpec((None, 8, tc), lambda b, i, c: (b, jnp.maximum(i * (tm // 8) - 1, 0), c0 + c)),
                  pl.BlockSpec((None, tm, tc), lambda b, i, c: (b, i, c0 + c)),
                  pl.BlockSpec((None, 8, tc),
                               lambda b, i, c: (b, jnp.minimum((i + 1) * (tm // 8), nsub - 1), c0 + c)),
                  pl.BlockSpec((C_CONV, tc), lambda b, i, c: (0, c0 + c))],
        out_specs=pl.BlockSpec((None, tm, tc), lambda b, i, c: (b, i, c)),
        out_shape=jax.ShapeDtypeStruct((bsz, length, width), BF16),
        compiler_params=_params(("parallel", "parallel", "parallel")),
        name="gdn_qkv",
    )(p, p, p, conv_w)


def _gdn_masks():
    t = np.arange(DCHUNK)
    same = (t[:, None] // C_CHUNK) == (t[None, :] // C_CHUNK)
    le, lt = t[None, :] <= t[:, None], t[None, :] < t[:, None]
    masks = [same & le, same & lt, same & le.T, same & lt.T]
    m = 1
    while m < C_CHUNK:
        masks.append(((t[:, None] // (2 * m)) == (t[None, :] // (2 * m))) & ((t[:, None] // m) != (t[None, :] // m)))
        m *= 2
    return jnp.asarray(np.stack(masks), F32)


def _block_diag2(t):
    z = jnp.zeros((t.shape[0], HEAD_DIM), t.dtype)
    return jnp.concatenate([jnp.concatenate([t[:, :HEAD_DIM], z], axis=1),
                            jnp.concatenate([z, t[:, HEAD_DIM:]], axis=1)], axis=0)


def _gdn_a_kernel(q_ref, k_ref, gcol_ref, grow_ref, mask_ref, lmask_ref, mb_ref, qk_ref, qd_ref, kd_ref,
                  gl_ref, a_scr, x_scr):
    j = pl.program_id(1)
    shift = lax.rem(HEAD_DIM - 8 * j, HEAD_DIM)
    nlvl = lmask_ref.shape[0]
    ndc = q_ref.shape[0] // DCHUNK
    rid = lax.broadcasted_iota(jnp.int32, (DCHUNK, DCHUNK), 0)
    cid = lax.broadcasted_iota(jnp.int32, (DCHUNK, DCHUNK), 1)
    eye = (rid == cid).astype(F32)
    first = rid[:, 0:1] < C_CHUNK
    brows = []
    for dc in range(ndc):
        rows = slice(dc * DCHUNK, (dc + 1) * DCHUNK)
        q, k = q_ref[rows, :], k_ref[rows, :]
        kk, qk = _dot_nt(k, k), _dot_nt(q, k)
        qf, kf = q.astype(F32), k.astype(F32)
        gc8 = pltpu.roll(gcol_ref[rows, :], shift, 1)
        gr = grow_ref[:, rows]
        gl_rows = []
        for d in range(2):
            causal, strict = mask_ref[2 * d], mask_ref[2 * d + 1]
            pair_a, pair_x = [], []
            for e in range(2):
                u = 2 * d + e
                gcc, bcol = gc8[:, u:u + 1], gc8[:, 4 + u:5 + u]
                gcr = gr[u:u + 1, :]
                dec = jnp.exp(jnp.where(causal > 0, gcc - gcr, 0.0)) * causal
                a = strict * (bcol * kk * dec)
                pair_a.append(a.astype(BF16))
                pair_x.append(eye - a * mask_ref[4])
                qk_ref[u, rows, :] = (qk * dec).astype(BF16)
                qd_ref[u, rows, :] = (qf * jnp.exp(gcc)).astype(BF16)
                l0, l1 = (C_CHUNK - 1, DCHUNK - 1) if d == 0 else (0, C_CHUNK)
                gl0, gl1 = gcc[l0:l0 + 1, :], gcc[l1:l1 + 1, :]
                glast = jnp.where(first, gl0, gl1)
                kd_ref[u, rows, :] = (kf * jnp.exp(glast - gcc)).astype(BF16)
                gl_rows += [jnp.broadcast_to(jnp.exp(gl0), (1, HEAD_DIM)),
                            jnp.broadcast_to(jnp.exp(gl1), (1, HEAD_DIM))]
            a_scr[2 * dc + d] = jnp.concatenate(pair_a, axis=1)
            x_scr[2 * dc + d] = jnp.concatenate(pair_x, axis=1)
            brows.append(jnp.concatenate([gr[4 + 2 * d:5 + 2 * d, :], gr[5 + 2 * d:6 + 2 * d, :]], axis=1))
        gl_ref[dc] = jnp.concatenate(gl_rows, axis=0)
    for lvl in range(nlvl):
        lm = lmask_ref[lvl]
        xb = [x_scr[c].astype(BF16) for c in range(2 * ndc)]
        pm = [_dot(a_scr[c] * lm, _block_diag2(xb[c])).astype(BF16) for c in range(2 * ndc)]
        for c in range(2 * ndc):
            x_scr[c] = x_scr[c] - _dot(xb[c], _block_diag2(pm[c]))
    for c in range(2 * ndc):
        dc, d = c // 2, c % 2
        mbp = (x_scr[c] * brows[c]).astype(BF16)
        rows = slice(dc * DCHUNK, (dc + 1) * DCHUNK)
        mb_ref[2 * d, rows, :] = mbp[:, :HEAD_DIM]
        mb_ref[2 * d + 1, rows, :] = mbp[:, HEAD_DIM:]


def _gdn_a(qn, kn, gcol, grow, lblk=512):
    bsz, length, kw = qn.shape
    kh = kw // HEAD_DIM
    lblk = min(lblk, length)
    masks = _gdn_masks()
    lmasks = jnp.concatenate([masks[5:], masks[5:]], axis=2).astype(BF16)
    masks = masks[:5]
    npair = 2 * (lblk // DCHUNK)
    unit = jax.ShapeDtypeStruct((bsz, kh, 4, length, HEAD_DIM), BF16)
    unit_spec = pl.BlockSpec((None, None, 4, lblk, HEAD_DIM), lambda b, j, n: (b, j, 0, n, 0))
    return pl.pallas_call(
        _gdn_a_kernel,
        grid=(bsz, kh, length // lblk),
        in_specs=[pl.BlockSpec((None, lblk, HEAD_DIM), lambda b, j, n: (b, n, j)),
                  pl.BlockSpec((None, lblk, HEAD_DIM), lambda b, j, n: (b, n, j)),
                  pl.BlockSpec((None, lblk, HEAD_DIM), lambda b, j, n: (b, n, 0)),
                  pl.BlockSpec((None, 8, lblk), lambda b, j, n: (b, j, n)),
                  pl.BlockSpec(masks.shape, lambda b, j, n: (0, 0, 0)),
                  pl.BlockSpec(lmasks.shape, lambda b, j, n: (0, 0, 0))],
        out_specs=[unit_spec, unit_spec, unit_spec, unit_spec,
                   pl.BlockSpec((None, None, lblk // DCHUNK, 8, HEAD_DIM), lambda b, j, n: (b, j, n, 0, 0))],
        out_shape=[unit, unit, unit, unit,
                   jax.ShapeDtypeStruct((bsz, kh, length // DCHUNK, 8, HEAD_DIM), F32)],
        scratch_shapes=[pltpu.VMEM((npair, DCHUNK, 2 * HEAD_DIM), BF16),
                        pltpu.VMEM((npair, DCHUNK, 2 * HEAD_DIM), F32)],
        compiler_params=_params(("parallel", "parallel", "parallel")),
        name="gdn_local",
    )(qn, kn, gcol, grow, masks, lmasks)


def _gdn_b_kernel(*refs):
    s0_ref, of_ref, ob_ref, sfin_ref, st_scr = refs[16:]
    n = pl.program_id(2)
    kb = st_scr.shape[0]

    @pl.when(n == 0)
    def _():
        st_scr[...] = s0_ref[...]

    zeros = jnp.zeros((C_CHUNK, HEAD_DIM), BF16)
    o_refs = (of_ref, ob_ref)
    nchunk = refs[0].shape[0] // C_CHUNK
    chains = [(jj, d) for jj in range(kb) for d in range(2)]
    for step in range(nchunk):
        ks, sbs, vns = {}, {}, {}
        for jj, d in chains:
            k_ref = refs[8 * d]
            ci = step if d == 0 else nchunk - 1 - step
            r = slice(ci * C_CHUNK, (ci + 1) * C_CHUNK)
            sbs[jj, d] = st_scr[jj, d].astype(BF16)
            ks[jj, d] = _dot(k_ref[r, jj * HEAD_DIM:(jj + 1) * HEAD_DIM], sbs[jj, d]).astype(BF16)
        for jj, d in chains:
            _, v_ref, mb_ref, _, _, _, _, gr_ref = refs[8 * d:8 * d + 8]
            ci = step if d == 0 else nchunk - 1 - step
            dc, cc = ci // 2, ci % 2
            big = slice(dc * DCHUNK, (dc + 1) * DCHUNK)
            r = slice(ci * C_CHUNK, (ci + 1) * C_CHUNK)
            place = (lambda t: jnp.concatenate([t, zeros], axis=0)) if cc == 0 else (
                lambda t: jnp.concatenate([zeros, t], axis=0))
            for e in range(2):
                u = 2 * d + e
                mb = mb_ref[jj, e, r, :]
                egc = jnp.exp(gr_ref[8 * jj + u:8 * jj + u + 1, big])
                mwn = (-(mb.astype(F32) * egc)).astype(BF16)
                vcol = slice((2 * jj + e) * HEAD_DIM, (2 * jj + e + 1) * HEAD_DIM)
                rhs = jnp.concatenate([v_ref[big, vcol], place(ks[jj, d][:, e * HEAD_DIM:(e + 1) * HEAD_DIM])], axis=0)
                vns[jj, d, e] = _dot(jnp.concatenate([mb, mwn], axis=1), rhs).astype(BF16)
        for jj, d in chains:
            _, _, _, qk_ref, qd_ref, kd_ref, gl_ref, _ = refs[8 * d:8 * d + 8]
            ci = step if d == 0 else nchunk - 1 - step
            dc, cc = ci // 2, ci % 2
            r = slice(ci * C_CHUNK, (ci + 1) * C_CHUNK)
            place = (lambda t: jnp.concatenate([t, zeros], axis=0)) if cc == 0 else (
                lambda t: jnp.concatenate([zeros, t], axis=0))
            upd, gls = [], []
            for e in range(2):
                u = 2 * d + e
                vn = vns[jj, d, e]
                lhs = jnp.concatenate([qd_ref[jj, e, r, :], qk_ref[jj, e, r, :]], axis=1)
                rhs = jnp.concatenate([sbs[jj, d][:, e * HEAD_DIM:(e + 1) * HEAD_DIM], place(vn)], axis=0)
                ocol = slice((2 * jj + e) * HEAD_DIM, (2 * jj + e + 1) * HEAD_DIM)
                o_refs[d][r, ocol] = _dot(lhs, rhs)
                upd.append(_dot_tn(kd_ref[jj, e, r, :], vn))
                gls.append(gl_ref[jj, dc, 2 * u + cc:2 * u + cc + 1, :])
            st_scr[jj, d] = st_scr[jj, d] * jnp.concatenate(gls, axis=1) + jnp.concatenate(upd, axis=1)

    @pl.when(n == pl.num_programs(2) - 1)
    def _():
        sfin_ref[...] = st_scr[...]


def _gdn_b(kn, vv, mb, qk, qd, kd, gl, grow, s0, lblk=256, kb=2):
    bsz, length, kw = kn.shape
    kh = kw // HEAD_DIM
    lblk = min(lblk, length)
    nb = length // lblk
    ndc = lblk // DCHUNK

    def specs(d):
        blk = (lambda n: nb - 1 - n) if d else (lambda n: n)
        unit = pl.BlockSpec((None, kb, 2, lblk, HEAD_DIM), lambda b, j, n: (b, j, d, blk(n), 0))
        return [pl.BlockSpec((None, lblk, kb * HEAD_DIM), lambda b, j, n: (b, blk(n), j)),
                pl.BlockSpec((None, lblk, 2 * kb * HEAD_DIM), lambda b, j, n: (b, blk(n), j)),
                unit, unit, unit, unit,
                pl.BlockSpec((None, kb, ndc, 8, HEAD_DIM), lambda b, j, n: (b, j, blk(n), 0, 0)),
                pl.BlockSpec((None, 8 * kb, lblk), lambda b, j, n: (b, j, blk(n)))]

    st_spec = pl.BlockSpec((None, kb, 2, HEAD_DIM, 2 * HEAD_DIM), lambda b, j, n: (b, j, 0, 0, 0))
    o_shape = jax.ShapeDtypeStruct((bsz, length, 2 * kw), F32)
    args = [kn, vv, mb, qk, qd, kd, gl, grow]
    return pl.pallas_call(
        _gdn_b_kernel,
        grid=(bsz, kh // kb, nb),
        in_specs=specs(0) + specs(1) + [st_spec],
        out_specs=[pl.BlockSpec((None, lblk, 2 * kb * HEAD_DIM), lambda b, j, n: (b, n, j)),
                   pl.BlockSpec((None, lblk, 2 * kb * HEAD_DIM), lambda b, j, n: (b, nb - 1 - n, j)),
                   st_spec],
        out_shape=[o_shape, o_shape, jax.ShapeDtypeStruct(s0.shape, F32)],
        scratch_shapes=[pltpu.VMEM((kb, 2, HEAD_DIM, 2 * HEAD_DIM), F32)],
        compiler_params=_params(("parallel", "parallel", "arbitrary")),
        name="gdn_scan",
    )(*args, *args, s0)


def _gate_layout(hv):
    perm, src = [], []
    for j in range(hv // 2):
        for i in range(8):
            u = i % 4
            d, e = u // 2, u % 2
            perm.append((0 if i < 4 else 2 * hv) + d * hv + 2 * j + e)
            src.append(d * hv + 2 * j + e if i < 4 else -1)
    return np.asarray(perm), np.asarray(src)


def kernel(x, c, ctx, c_ctx, w_ada, b_ada, norm_w, ev_w_in, ev_lb, ev_a_norm, ev_pool_w, ev_pool_scale,
           ev_w_out, od_w_in, od_conv, od_A_log, od_dt_bias, od_norm, od_w_out, ffn_w13, ffn_w2):
    bsz, seq, d = x.shape
    depth = w_ada.shape[0]
    ctx_len = ctx.shape[1]
    a_width = ev_lb.shape[2]
    heads_a = a_width // HEAD_DIM
    assert bsz + 1 <= 8

    cvec = jnp.zeros((8, d), F32).at[0].set(c_ctx).at[1:1 + bsz].set(c)
    mod = _ada(cvec, w_ada, b_ada).reshape(depth, 8, 6, d)
    lat_row = lambda b: b + 1
    ctx_row = lambda b: 0
    lb_all = jnp.cumsum(jax.nn.softmax(ev_lb.astype(F32), axis=1), axis=1)

    for layer in range(depth):
        need_ctx = layer < depth - 1
        j = layer // 2
        nw = norm_w[layer]
        w13 = ffn_w13[layer].astype(BF16)
        w2 = ffn_w2[layer].astype(BF16)
        if layer % 2 == 0:
            w_in = ev_w_in[j].astype(BF16)
            lb = lb_all[:, layer]
            p_l = _proj(x, mod, layer, lat_row, nw[0], w_in)
            p_c = _proj(ctx, mod, layer, ctx_row, nw[0], w_in)
            s0 = jnp.zeros((2, bsz, heads_a, HEAD_DIM, HEAD_DIM), F32)
            oc_f, oc_b, s_ctx = _hgrn(p_c, lb, s0)
            ol_f, ol_b, _ = _hgrn(p_l, lb, s_ctx)
            x = _even_out(ol_f, ol_b, p_l, x, mod, layer, lat_row, ev_a_norm[j], ev_pool_w[j],
                          ev_pool_scale[j], ev_w_out[j], nw[1], GRID_W)
            if need_ctx:
                ctx = _even_out(oc_f, oc_b, p_c, ctx, mod, layer, ctx_row, ev_a_norm[j], ev_pool_w[j],
                                ev_pool_scale[j], ev_w_out[j], nw[1], ctx_len)
        else:
            assert not need_ctx
            hv = od_A_log.shape[2]
            kh = hv // 2
            kw, vw = kh * HEAD_DIM, hv * HEAD_DIM
            nmain = 2 * kw + 2 * vw
            w_main = od_w_in[j][:, :nmain].astype(BF16)
            perm, src = _gate_layout(hv)
            wg = _split_bf16(od_w_in[j][:, nmain:][:, perm])
            lane_param = lambda t: jnp.where(src >= 0, t.reshape(-1)[np.maximum(src, 0)], 0.0).reshape(1, -1)
            alog_lane, dtb_lane = lane_param(od_A_log[j]), lane_param(od_dt_bias[j])
            conv_w = od_conv[j]

            def mix(p, pg, s0):
                gcol, grow = _gates(pg, alog_lane, dtb_lane)
                qn = _qkv(p, 0, kw, conv_w, True, HEAD_DIM ** -0.5)
                kn = _qkv(p, kw, kw, conv_w, True, 1.0)
                vv = _qkv(p, 2 * kw, vw, conv_w, False, 1.0)
                mb, qk, qd, kd, gl = _gdn_a(qn, kn, gcol, grow)
                return _gdn_b(kn, vv, mb, qk, qd, kd, gl, grow, s0)

            p_l, g_l = _proj(x, mod, layer, lat_row, nw[0], w_main, wg)
            p_c, g_c = _proj(ctx, mod, layer, ctx_row, nw[0], w_main, wg)
            s0 = jnp.zeros((bsz, kh, 2, HEAD_DIM, 2 * HEAD_DIM), F32)
            _, _, s_ctx = mix(p_c, g_c, s0)
            o_f, o_b, _ = mix(p_l, g_l, s_ctx)
            x = _odd_out(o_f, o_b, p_l, 2 * kw + vw, x, mod, layer, lat_row, od_norm[j], od_w_out[j], nw[1])
        x = _ffn(x, mod, layer, lat_row, nw[2], nw[3], w13, w2)
        if need_ctx:
            ctx = _ffn(ctx, mod, layer, ctx_row, nw[2], nw[3], w13, w2)
    return x
```

```python
import functools

import numpy as np
import jax
import jax.numpy as jnp
from jax import lax
from jax.experimental import pallas as pl
from jax.experimental.pallas import tpu as pltpu

F32 = jnp.float32
BF16 = jnp.bfloat16
EPS = 1e-6

HEAD_DIM = 128
GRID_W = 64
POOL_WINDOWS = (2, 4, 8, 16)
A_CHUNK = 32
C_CONV = 4
C_CHUNK = 64
DCHUNK = 2 * C_CHUNK
VMEM_LIMIT = 56 * 1024 * 1024

NT = (((1,), (1,)), ((), ()))
TN = (((0,), (0,)), ((), ()))


def _dot(a, b):
    return jnp.dot(a, b, preferred_element_type=F32)


def _dot_nt(a, b):
    return lax.dot_general(a, b, NT, preferred_element_type=F32)


def _dot_tn(a, b):
    return lax.dot_general(a, b, TN, preferred_element_type=F32)


def _silu(x):
    return x * jax.nn.sigmoid(x)


def _rms(x):
    return x * lax.rsqrt(jnp.mean(x * x, axis=-1, keepdims=True) + EPS)


def _split_bf16(x):
    hi = x.astype(BF16)
    lo = (x - hi.astype(F32)).astype(BF16)
    return hi, lo


ROW_BLOCK = 16


def _row_blocks(nrows, body):
    def step(i, carry):
        body(pl.ds(pl.multiple_of(i * ROW_BLOCK, ROW_BLOCK), ROW_BLOCK))
        return carry
    lax.fori_loop(0, nrows // ROW_BLOCK, step, 0)


def _norm_modulate(x_ref, nw, shift, scale, hi_ref, lo_ref=None):
    def body(rows):
        h = (_rms(x_ref[rows, :]) * nw) * (1.0 + scale) + shift
        hh = h.astype(BF16)
        hi_ref[rows, :] = hh
        if lo_ref is not None:
            lo_ref[rows, :] = (h - hh.astype(F32)).astype(BF16)
    _row_blocks(x_ref.shape[0], body)


def _gated_norm_residual(x_ref, y_ref, gate, nw, o_ref):
    def body(rows):
        o_ref[rows, :] = x_ref[rows, :] + gate * (_rms(y_ref[rows, :]) * nw)
    _row_blocks(x_ref.shape[0], body)


def _params(sem):
    return pltpu.CompilerParams(dimension_semantics=sem, vmem_limit_bytes=VMEM_LIMIT)


def _ada_kernel(c_ref, w_ref, b_ref, o_ref):
    s = _silu(c_ref[...]).astype(BF16)
    o_ref[...] = _dot(s, w_ref[...].astype(BF16)) + b_ref[...]


def _ada(cvec, w_ada, b_ada):
    depth, d, n = w_ada.shape
    tn = 1024
    return pl.pallas_call(
        _ada_kernel,
        grid=(depth, n // tn),
        in_specs=[
            pl.BlockSpec((8, d), lambda l, j: (0, 0)),
            pl.BlockSpec((None, d, tn), lambda l, j: (l, 0, j)),
            pl.BlockSpec((None, 1, tn), lambda l, j: (l, 0, j)),
        ],
        out_specs=pl.BlockSpec((None, 8, tn), lambda l, j: (l, 0, j)),
        out_shape=jax.ShapeDtypeStruct((depth, 8, n), F32),
        compiler_params=_params(("parallel", "parallel")),
        name="ada",
    )(cvec, w_ada, b_ada.reshape(depth, 1, n))


def _proj_kernel(x_ref, mod_ref, nw_ref, w_ref, *rest, has_gate):
    if has_gate:
        wgh_ref, wgl_ref, o_ref, og_ref, h_scr, hl_scr = rest
    else:
        o_ref, h_scr = rest
        hl_scr = None
    j = pl.program_id(2)

    @pl.when(j == 0)
    def _():
        _norm_modulate(x_ref, nw_ref[...], mod_ref[0:1, :], mod_ref[1:2, :], h_scr, hl_scr)
        if has_gate:
            hh, hl = h_scr[...], hl_scr[...]
            og_ref[...] = _dot(hh, wgh_ref[...]) + _dot(hl, wgh_ref[...]) + _dot(hh, wgl_ref[...])

    o_ref[...] = _dot(h_scr[...], w_ref[...]).astype(o_ref.dtype)


def _proj(x, mod, layer, row_of_batch, nw, w, wg=None, tm=1024, tn=1024):
    bsz, length, d = x.shape
    n = w.shape[1]
    tm = min(tm, length)
    has_gate = wg is not None
    in_specs = [
        pl.BlockSpec((None, tm, d), lambda b, i, j: (b, i, 0)),
        pl.BlockSpec((None, None, 6, d), lambda b, i, j: (layer, row_of_batch(b), 0, 0)),
        pl.BlockSpec((1, d), lambda b, i, j: (0, 0)),
        pl.BlockSpec((d, tn), lambda b, i, j: (0, j)),
    ]
    out_specs = [pl.BlockSpec((None, tm, tn), lambda b, i, j: (b, i, j))]
    out_shape = [jax.ShapeDtypeStruct((bsz, length, n), BF16)]
    args = [x, mod, nw.reshape(1, d), w]
    if has_gate:
        ng = wg[0].shape[1]
        in_specs += [pl.BlockSpec((d, ng), lambda b, i, j: (0, 0))] * 2
        out_specs.append(pl.BlockSpec((None, tm, ng), lambda b, i, j: (b, i, 0)))
        out_shape.append(jax.ShapeDtypeStruct((bsz, length, ng), F32))
        args += list(wg)
    res = pl.pallas_call(
        functools.partial(_proj_kernel, has_gate=has_gate),
        grid=(bsz, length // tm, n // tn),
        in_specs=in_specs,
        out_specs=out_specs,
        out_shape=out_shape,
        scratch_shapes=[pltpu.VMEM((tm, d), BF16)] * (2 if has_gate else 1),
        compiler_params=_params(("parallel", "parallel", "arbitrary")),
        name="proj",
    )(*args)
    return res if has_gate else res[0]


def _hgrn_kernel(qf_ref, ff_ref, if_ref, qb_ref, fb_ref, ib_ref, lb_ref, s0_ref, trif_ref, trib_ref,
                 of_ref, ob_ref, sfin_ref, st_scr):
    n = pl.program_id(2)
    lblk = qf_ref.shape[0]
    nchunk = lblk // A_CHUNK
    hb = qf_ref.shape[1] // HEAD_DIM

    @pl.when(n == 0)
    def _():
        st_scr[...] = s0_ref[...]

    dirs = ((qf_ref, ff_ref, if_ref, of_ref, trif_ref), (qb_ref, fb_ref, ib_ref, ob_ref, trib_ref))
    pre = {}
    for d, (q_ref, f_ref, i_ref, o_ref, tri_ref) in enumerate(dirs):
        tri = tri_ref[...]
        for hh in range(hb):
            c = slice(hh * HEAD_DIM, (hh + 1) * HEAD_DIM)
            lb = lb_ref[d][:, c]
            s = f_ref[:, c].astype(F32)
            logf = jnp.log(lb + (1.0 - lb) * jax.nn.sigmoid(s))
            k = (1.0 - lb) * jax.nn.sigmoid(-s)
            qa = _silu(q_ref[:, c].astype(F32))
            v = i_ref[:, c]
            hi, lo = _split_bf16(logf)
            b = _dot(tri, hi) + _dot(tri, lo)
            b3 = b.reshape(nchunk, A_CHUNK, HEAD_DIM)
            mid = A_CHUNK // 2 - 1 if d == 0 else A_CHUNK // 2
            last = A_CHUNK - 1 if d == 0 else 0
            bmid = jnp.broadcast_to(b3[:, mid:mid + 1, :], b3.shape).reshape(b.shape)
            blast = jnp.broadcast_to(b3[:, last:last + 1, :], b3.shape).reshape(b.shape)
            qs = (qa * jnp.exp(b - bmid)).astype(BF16)
            ks = (k * jnp.exp(bmid - b)).astype(BF16)
            scores = jnp.where(tri > 0, _dot_nt(qs, ks), 0.0)
            o_intra = _dot(scores.astype(BF16), v)
            q_in = (qa * jnp.exp(b)).astype(BF16)
            k_out = (k * jnp.exp(blast - b)).astype(BF16)
            pre[d, hh] = (o_intra, q_in, k_out, jnp.exp(blast), v)
    for step in range(nchunk):
        for d in range(2):
            ci = step if d == 0 else nchunk - 1 - step
            r = slice(ci * A_CHUNK, (ci + 1) * A_CHUNK)
            for hh in range(hb):
                o_intra, q_in, k_out, dec, v = pre[d, hh]
                st = st_scr[d, hh]
                dirs[d][3][r, hh * HEAD_DIM:(hh + 1) * HEAD_DIM] = (
                    o_intra[r] + _dot_nt(q_in[r], st.astype(BF16))).astype(BF16)
                st_scr[d, hh] = st * dec[ci * A_CHUNK:ci * A_CHUNK + 1] + _dot_tn(v[r], k_out[r])

    @pl.when(n == pl.num_programs(2) - 1)
    def _():
        sfin_ref[...] = st_scr[...]


def _chunk_tri(lblk, chunk, upper):
    t = np.arange(lblk)
    same = (t[:, None] // chunk) == (t[None, :] // chunk)
    tri = (t[None, :] >= t[:, None]) if upper else (t[None, :] <= t[:, None])
    return jnp.asarray(same & tri, BF16)


def _hgrn(p, lb, s0, lblk=256, hb=4):
    bsz, length, _ = p.shape
    heads = lb.shape[1] // HEAD_DIM
    hg = heads // hb
    lblk = min(lblk, length)
    nb = length // lblk
    blk = lambda col, rev: pl.BlockSpec(
        (None, lblk, hb * HEAD_DIM),
        (lambda b, h, n: (b, nb - 1 - n, col * hg + h)) if rev else (lambda b, h, n: (b, n, col * hg + h)))
    oblk = lambda rev: pl.BlockSpec(
        (None, lblk, hb * HEAD_DIM), (lambda b, h, n: (b, nb - 1 - n, h)) if rev else (lambda b, h, n: (b, n, h)))
    st_spec = pl.BlockSpec((2, None, hb, HEAD_DIM, HEAD_DIM), lambda b, h, n: (0, b, h, 0, 0))
    tri_spec = pl.BlockSpec((lblk, lblk), lambda b, h, n: (0, 0))
    width = heads * HEAD_DIM
    return pl.pallas_call(
        _hgrn_kernel,
        grid=(bsz, hg, nb),
        in_specs=[blk(0, False), blk(1, False), blk(3, False), blk(0, True), blk(2, True), blk(3, True),
                  pl.BlockSpec((2, None, 1, hb * HEAD_DIM), lambda b, h, n: (0, h, 0, 0)),
                  st_spec, tri_spec, tri_spec],
        out_specs=[oblk(False), oblk(True), st_spec],
        out_shape=[jax.ShapeDtypeStruct((bsz, length, width), BF16),
                   jax.ShapeDtypeStruct((bsz, length, width), BF16),
                   jax.ShapeDtypeStruct(s0.shape, F32)],
        scratch_shapes=[pltpu.VMEM((2, hb, HEAD_DIM, HEAD_DIM), F32)],
        compiler_params=_params(("parallel", "parallel", "arbitrary")),
        name="hgrn2",
    )(p, p, p, p, p, p, lb.reshape(2, hg, 1, hb * HEAD_DIM), s0,
      _chunk_tri(lblk, A_CHUNK, False), _chunk_tri(lblk, A_CHUNK, True))


def _even_out_kernel(of_ref, ob_ref, g_ref, u_ref, x_ref, mod_ref, an_ref, band_ref, icnt_ref, pw_ref,
                     ps_ref, wout_ref, nw_ref, o_ref):
    o = of_ref[...].astype(F32) + ob_ref[...].astype(F32)
    g = g_ref[...].astype(F32)
    u = u_ref[...]
    heads = o.shape[1] // HEAD_DIM
    parts = []
    for h in range(heads):
        c = slice(h * HEAD_DIM, (h + 1) * HEAD_DIM)
        parts.append((_rms(o[:, c]) * an_ref[...] * _silu(g[:, c])).astype(BF16))
    gd = u.shape[1] // len(POOL_WINDOWS)
    for gi in range(len(POOL_WINDOWS)):
        c = slice(gi * gd, (gi + 1) * gd)
        ug = u[:, c]
        dlt = _dot(band_ref[gi], ug) * icnt_ref[gi] - ug.astype(F32)
        parts.append((_dot(dlt.astype(BF16), pw_ref[gi]) * ps_ref[:, c]).astype(BF16))
    y = _dot(jnp.concatenate(parts, axis=1), wout_ref[...])
    o_ref[...] = x_ref[...] + mod_ref[2:3, :] * (_rms(y) * nw_ref[...])


def _pool_consts(tm, roww):
    t = np.arange(tm)
    row, pos = t // roww, t % roww
    bands, icnts = [], []
    for win in POOL_WINDOWS:
        lo = np.clip(pos - win // 2, 0, roww - 1)
        hi = np.clip(pos + win - 1 - win // 2, 0, roww - 1)
        m = (row[:, None] == row[None, :]) & (pos[None, :] >= lo[:, None]) & (pos[None, :] <= hi[:, None])
        bands.append(m)
        icnts.append(np.broadcast_to((1.0 / (hi - lo + 1))[:, None], (tm, 256)))
    return jnp.asarray(np.stack(bands), BF16), jnp.asarray(np.stack(icnts), F32)


def _even_out(o_f, o_b, p, x, mod, layer, row_of_batch, a_norm, pool_w, pool_scale, w_out, nw, roww):
    bsz, length, d = x.shape
    width = o_f.shape[2]
    tm = 256
    assert length % tm == 0 and tm % roww == 0
    band, icnt = _pool_consts(tm, roww)
    ng, gd = pool_w.shape[0], pool_w.shape[1]
    gcol, ucol = p.shape[2] // width - 2, p.shape[2] // width - 1
    tok = lambda w, col: pl.BlockSpec((None, tm, w), lambda b, i: (b, i, col))
    const = lambda shape: pl.BlockSpec(shape, lambda b, i: (0,) * len(shape))
    return pl.pallas_call(
        _even_out_kernel,
        grid=(bsz, length // tm),
        in_specs=[tok(width, 0), tok(width, 0), tok(width, gcol), tok(width, ucol), tok(d, 0),
                  pl.BlockSpec((None, None, 6, d), lambda b, i: (layer, row_of_batch(b), 0, 0)),
                  const((1, HEAD_DIM)), const((ng, tm, tm)), const((ng, tm, 256)), const((ng, gd, gd)),
                  const((1, width)), const((2 * width, d)), const((1, d))],
        out_specs=tok(d, 0),
        out_shape=jax.ShapeDtypeStruct(x.shape, F32),
        compiler_params=_params(("parallel", "parallel")),
        name="even_out",
    )(o_f, o_b, p, p, x, mod, a_norm.reshape(1, HEAD_DIM), band, icnt, pool_w.astype(BF16),
      pool_scale.reshape(1, width), w_out.astype(BF16), nw.reshape(1, d))


def _odd_out_kernel(of_ref, ob_ref, z_ref, x_ref, mod_ref, hn_ref, w_ref, nw_ref, o_ref, *, tk):
    y = None
    for g in range(of_ref.shape[1] // tk):
        parts = []
        for h in range(tk // HEAD_DIM):
            c = slice(g * tk + h * HEAD_DIM, g * tk + (h + 1) * HEAD_DIM)
            o = of_ref[:, c].astype(F32) + ob_ref[:, c].astype(F32)
            parts.append((_rms(o) * hn_ref[...] * _silu(z_ref[:, c].astype(F32))).astype(BF16))
        part = _dot(jnp.concatenate(parts, axis=1), w_ref[g * tk:(g + 1) * tk, :])
        y = part if y is None else y + part
    o_ref[...] = x_ref[...] + mod_ref[2:3, :] * (_rms(y) * nw_ref[...])


def _odd_out(o_f, o_b, p, zcol0, x, mod, layer, row_of_batch, head_norm, w_out, nw, tm=256, tk=1024):
    bsz, length, d = x.shape
    vw = o_f.shape[2]
    tm = min(tm, length)
    zoff = zcol0 // vw
    return pl.pallas_call(
        functools.partial(_odd_out_kernel, tk=tk),
        grid=(bsz, length // tm),
        in_specs=[pl.BlockSpec((None, tm, vw), lambda b, i: (b, i, 0)),
                  pl.BlockSpec((None, tm, vw), lambda b, i: (b, i, 0)),
                  pl.BlockSpec((None, tm, vw), lambda b, i: (b, i, zoff)),
                  pl.BlockSpec((None, tm, d), lambda b, i: (b, i, 0)),
                  pl.BlockSpec((None, None, 6, d), lambda b, i: (layer, row_of_batch(b), 0, 0)),
                  pl.BlockSpec((1, HEAD_DIM), lambda b, i: (0, 0)),
                  pl.BlockSpec((vw, d), lambda b, i: (0, 0), pipeline_mode=pl.Buffered(1)),
                  pl.BlockSpec((1, d), lambda b, i: (0, 0))],
        out_specs=pl.BlockSpec((None, tm, d), lambda b, i: (b, i, 0)),
        out_shape=jax.ShapeDtypeStruct(x.shape, F32),
        compiler_params=_params(("parallel", "parallel")),
        name="odd_out",
    )(o_f, o_b, p, x, mod, head_norm.reshape(1, HEAD_DIM), w_out.astype(BF16), nw.reshape(1, d))


def _ffn_kernel(x_ref, mod_ref, nwa_ref, nwb_ref, w1_ref, w3_ref, w2_ref, o_ref, h_scr, acc_scr):
    j = pl.program_id(2)

    @pl.when(j == 0)
    def _():
        _norm_modulate(x_ref, nwa_ref[...], mod_ref[3:4, :], mod_ref[4:5, :], h_scr)

    h = h_scr[...]
    t = (_silu(_dot(h, w1_ref[...])) * _dot(h, w3_ref[...])).astype(BF16)
    y = _dot(t, w2_ref[...])

    @pl.when(j == 0)
    def _():
        acc_scr[...] = y

    @pl.when(j > 0)
    def _():
        acc_scr[...] += y

    @pl.when(j == pl.num_programs(2) - 1)
    def _():
        _gated_norm_residual(x_ref, acc_scr, mod_ref[5:6, :], nwb_ref[...], o_ref)


def _ffn(x, mod, layer, row_of_batch, nwa, nwb, w13, w2, tm=512, th=512):
    bsz, length, d = x.shape
    hidden = w2.shape[0]
    tm = min(tm, length)
    nh = hidden // th
    return pl.pallas_call(
        _ffn_kernel,
        grid=(bsz, length // tm, nh),
        in_specs=[pl.BlockSpec((None, tm, d), lambda b, i, j: (b, i, 0)),
                  pl.BlockSpec((None, None, 6, d), lambda b, i, j: (layer, row_of_batch(b), 0, 0)),
                  pl.BlockSpec((1, d), lambda b, i, j: (0, 0)),
                  pl.BlockSpec((1, d), lambda b, i, j: (0, 0)),
                  pl.BlockSpec((d, th), lambda b, i, j: (0, j)),
                  pl.BlockSpec((d, th), lambda b, i, j: (0, nh + j)),
                  pl.BlockSpec((th, d), lambda b, i, j: (j, 0))],
        out_specs=pl.BlockSpec((None, tm, d), lambda b, i, j: (b, i, 0)),
        out_shape=jax.ShapeDtypeStruct(x.shape, F32),
        scratch_shapes=[pltpu.VMEM((tm, d), BF16), pltpu.VMEM((tm, d), F32)],
        compiler_params=_params(("parallel", "parallel", "arbitrary")),
        name="ffn",
    )(x, mod, nwa.reshape(1, d), nwb.reshape(1, d), w13, w13, w2)


def _gate_kernel(pg_ref, alog_ref, dtb_ref, trif_ref, trib_ref, gcol_ref, grow_ref):
    x = pg_ref[...]
    i8 = lax.broadcasted_iota(jnp.int32, x.shape, 1) & 7
    xa = x + dtb_ref[...]
    softplus = jnp.maximum(xa, 0.0) + jnp.log(1.0 + jnp.exp(-jnp.abs(xa)))
    g = -jnp.exp(alog_ref[...]) * softplus
    g1 = g.astype(BF16)
    r1 = g - g1.astype(F32)
    g2 = r1.astype(BF16)
    g3 = (r1 - g2.astype(F32)).astype(BF16)
    trif, trib = trif_ref[...], trib_ref[...]
    cf = _dot(trif, g1) + _dot(trif, g2) + _dot(trif, g3)
    cb = _dot(trib, g1) + _dot(trib, g2) + _dot(trib, g3)
    out = jnp.where(i8 < 2, cf, jnp.where(i8 < 4, cb, jax.nn.sigmoid(x)))
    gcol_ref[...] = out
    grow_ref[...] = out.T


def _gates(pg, alog_lane, dtb_lane, tm=256):
    bsz, length, nl = pg.shape
    tm = min(tm, length)
    const = lambda shape: pl.BlockSpec(shape, lambda b, i: (0,) * len(shape))
    return pl.pallas_call(
        _gate_kernel,
        grid=(bsz, length // tm),
        in_specs=[pl.BlockSpec((None, tm, nl), lambda b, i: (b, i, 0)), const((1, nl)), const((1, nl)),
                  const((tm, tm)), const((tm, tm))],
        out_specs=[pl.BlockSpec((None, tm, nl), lambda b, i: (b, i, 0)),
                   pl.BlockSpec((None, nl, tm), lambda b, i: (b, 0, i))],
        out_shape=[jax.ShapeDtypeStruct((bsz, length, nl), F32), jax.ShapeDtypeStruct((bsz, nl, length), F32)],
        compiler_params=_params(("parallel", "parallel")),
        name="gdn_gates",
    )(pg, alog_lane, dtb_lane, _chunk_tri(tm, C_CHUNK, False), _chunk_tri(tm, C_CHUNK, True))


def _qkv_kernel(prev_ref, x_ref, next_ref, cw_ref, shift_ref, o_ref, *, l2, scale):
    i = pl.program_id(1)
    tm = x_ref.shape[0]
    xb = x_ref[...]
    prev8 = jnp.where(i > 0, prev_ref[8:16, :].astype(F32), 0.0)
    next8 = jnp.where(i < pl.num_programs(1) - 1, next_ref[0:8, :].astype(F32), 0.0)
    sub = lax.broadcasted_iota(jnp.int32, (8, xb.shape[1]), 0)
    top8, bot8 = xb[0:8].astype(F32), xb[tm - 8:tm].astype(F32)
    acc = None
    nshift = 0
    for j in range(C_CONV):
        off = j - C_CONV // 2
        if off == 0:
            sh = xb.astype(F32)
        else:
            body = _dot(shift_ref[nshift], xb)
            nshift += 1
            if off < 0:
                fixed = jnp.where(sub < -off, pltpu.roll(prev8, -off, 0), pltpu.roll(top8, -off, 0))
                sh = jnp.concatenate([fixed, body[8:]], axis=0)
            else:
                fixed = jnp.where(sub >= 8 - off, pltpu.roll(next8, 8 - off, 0), pltpu.roll(bot8, 8 - off, 0))
                sh = jnp.concatenate([body[:tm - 8], fixed], axis=0)
        term = cw_ref[j:j + 1, :] * sh
        acc = term if acc is None else acc + term
    y = _silu(acc)
    if l2:
        parts = []
        for h in range(y.shape[1] // HEAD_DIM):
            yh = y[:, h * HEAD_DIM:(h + 1) * HEAD_DIM]
            parts.append(yh * (lax.rsqrt(jnp.sum(yh * yh, axis=-1, keepdims=True) + EPS) * scale))
        y = jnp.concatenate(parts, axis=1)
    o_ref[...] = y.astype(BF16)


def _qkv(p, col0, width, conv_w, l2, scale, tm=256, tc=1024):
    bsz, length, _ = p.shape
    tm = min(tm, length)
    c0 = col0 // tc
    halo = 16
    nsub = length // halo
    t = np.arange(tm)
    offs = [j - C_CONV // 2 for j in range(C_CONV) if j != C_CONV // 2]
    shifts = jnp.asarray(np.stack([t[None, :] == t[:, None] + off for off in offs]), BF16)
    return pl.pallas_call(
        functools.partial(_qkv_kernel, l2=l2, scale=scale),
        grid=(bsz, length // tm, width // tc),
        in_specs=[pl.BlockSpec((None, halo, tc),
                               lambda b, i, c: (b, jnp.maximum(i * (tm // halo) - 1, 0), c0 + c)),
                  pl.BlockSpec((None, tm, tc), lambda b, i, c: (b, i, c0 + c)),
                  pl.BlockSpec((None, halo, tc),
                               lambda b, i, c: (b, jnp.minimum((i + 1) * (tm // halo), nsub - 1), c0 + c)),
                  pl.BlockSpec((C_CONV, tc), lambda b, i, c: (0, c0 + c)),
                  pl.BlockSpec(shifts.shape, lambda b, i, c: (0, 0, 0))],
        out_specs=pl.BlockSpec((None, tm, tc), lambda b, i, c: (b, i, c)),
        out_shape=jax.ShapeDtypeStruct((bsz, length, width), BF16),
        compiler_params=_params(("parallel", "parallel", "parallel")),
        name="gdn_qkv",
    )(p, p, p, conv_w, shifts)


def _gdn_masks():
    t = np.arange(DCHUNK)
    same = (t[:, None] // C_CHUNK) == (t[None, :] // C_CHUNK)
    le, lt = t[None, :] <= t[:, None], t[None, :] < t[:, None]
    masks = [same & le, same & lt, same & le.T, same & lt.T]
    m = 1
    while m < C_CHUNK:
        masks.append(((t[:, None] // (2 * m)) == (t[None, :] // (2 * m))) & ((t[:, None] // m) != (t[None, :] // m)))
        m *= 2
    return jnp.asarray(np.stack(masks), F32)


def _block_diag2(t):
    z = jnp.zeros((t.shape[0], HEAD_DIM), t.dtype)
    return jnp.concatenate([jnp.concatenate([t[:, :HEAD_DIM], z], axis=1),
                            jnp.concatenate([z, t[:, HEAD_DIM:]], axis=1)], axis=0)


def _gdn_a_kernel(q_ref, k_ref, gcol_ref, grow_ref, mask_ref, lmask_ref, mb_ref, qk_ref, a_scr, x_scr):
    j = pl.program_id(1)
    shift = lax.rem(HEAD_DIM - 8 * j, HEAD_DIM)
    nlvl = lmask_ref.shape[0]
    ndc = q_ref.shape[0] // DCHUNK
    rid = lax.broadcasted_iota(jnp.int32, (DCHUNK, DCHUNK), 0)
    cid = lax.broadcasted_iota(jnp.int32, (DCHUNK, DCHUNK), 1)
    eye = (rid == cid).astype(F32)
    brows = []
    for dc in range(ndc):
        rows = slice(dc * DCHUNK, (dc + 1) * DCHUNK)
        q, k = q_ref[rows, :], k_ref[rows, :]
        kk, qk = _dot_nt(k, k), _dot_nt(q, k)
        gc8 = pltpu.roll(gcol_ref[rows, :], shift, 1)
        gr = grow_ref[:, rows]
        for d in range(2):
            causal, strict = mask_ref[2 * d], mask_ref[2 * d + 1]
            pair_a, pair_x = [], []
            for e in range(2):
                u = 2 * d + e
                gcc, bcol = gc8[:, u:u + 1], gc8[:, 4 + u:5 + u]
                gcr = gr[u:u + 1, :]
                dec = jnp.exp(jnp.where(causal > 0, gcc - gcr, 0.0)) * causal
                a = strict * (bcol * kk * dec)
                pair_a.append(a.astype(BF16))
                pair_x.append(eye - a * mask_ref[4])
                qk_ref[u, rows, :] = (qk * dec).astype(BF16)
            a_scr[2 * dc + d] = jnp.concatenate(pair_a, axis=1)
            x_scr[2 * dc + d] = jnp.concatenate(pair_x, axis=1)
            brows.append(jnp.concatenate([gr[4 + 2 * d:5 + 2 * d, :], gr[5 + 2 * d:6 + 2 * d, :]], axis=1))
    for lvl in range(nlvl):
        lm = lmask_ref[lvl]
        xb = [x_scr[c].astype(BF16) for c in range(2 * ndc)]
        pm = [_dot(a_scr[c] * lm, _block_diag2(xb[c])).astype(BF16) for c in range(2 * ndc)]
        for c in range(2 * ndc):
            x_scr[c] = x_scr[c] - _dot(xb[c], _block_diag2(pm[c]))
    for c in range(2 * ndc):
        dc, d = c // 2, c % 2
        mbp = (x_scr[c] * brows[c]).astype(BF16)
        rows = slice(dc * DCHUNK, (dc + 1) * DCHUNK)
        mb_ref[2 * d, rows, :] = mbp[:, :HEAD_DIM]
        mb_ref[2 * d + 1, rows, :] = mbp[:, HEAD_DIM:]


def _gdn_a(qn, kn, gcol, grow, lblk=512):
    bsz, length, kw = qn.shape
    kh = kw // HEAD_DIM
    lblk = min(lblk, length)
    masks = _gdn_masks()
    lmasks = jnp.concatenate([masks[5:], masks[5:]], axis=2).astype(BF16)
    masks = masks[:5]
    npair = 2 * (lblk // DCHUNK)
    unit = jax.ShapeDtypeStruct((bsz, kh, 4, length, HEAD_DIM), BF16)
    unit_spec = pl.BlockSpec((None, None, 4, lblk, HEAD_DIM), lambda b, j, n: (b, j, 0, n, 0))
    return pl.pallas_call(
        _gdn_a_kernel,
        grid=(bsz, kh, length // lblk),
        in_specs=[pl.BlockSpec((None, lblk, HEAD_DIM), lambda b, j, n: (b, n, j)),
                  pl.BlockSpec((None, lblk, HEAD_DIM), lambda b, j, n: (b, n, j)),
                  pl.BlockSpec((None, lblk, HEAD_DIM), lambda b, j, n: (b, n, 0)),
                  pl.BlockSpec((None, 8, lblk), lambda b, j, n: (b, j, n)),
                  pl.BlockSpec(masks.shape, lambda b, j, n: (0, 0, 0)),
                  pl.BlockSpec(lmasks.shape, lambda b, j, n: (0, 0, 0))],
        out_specs=[unit_spec, unit_spec],
        out_shape=[unit, unit],
        scratch_shapes=[pltpu.VMEM((npair, DCHUNK, 2 * HEAD_DIM), BF16),
                        pltpu.VMEM((npair, DCHUNK, 2 * HEAD_DIM), F32)],
        compiler_params=_params(("parallel", "parallel", "parallel")),
        name="gdn_local",
    )(qn, kn, gcol, grow, masks, lmasks)


def _gdn_b_kernel(*refs):
    nin = 7
    s0_ref, of_ref, ob_ref, sfin_ref, st_scr = refs[2 * nin:]
    n = pl.program_id(2)
    kb = st_scr.shape[0]

    @pl.when(n == 0)
    def _():
        st_scr[...] = s0_ref[...]

    zeros = jnp.zeros((C_CHUNK, HEAD_DIM), BF16)
    o_refs = (of_ref, ob_ref)
    nchunk = refs[0].shape[0] // C_CHUNK
    chains = [(jj, d) for jj in range(kb) for d in range(2)]
    gc8 = {}
    for jj, d in chains:
        shift = lax.rem(HEAD_DIM - 8 * (pl.program_id(1) * kb + jj), HEAD_DIM)
        gc8[jj, d] = pltpu.roll(refs[nin * d + 5][...], shift, 1)
    for step in range(nchunk):
        ks, sbs, vns = {}, {}, {}
        for jj, d in chains:
            k_ref = refs[nin * d + 1]
            ci = step if d == 0 else nchunk - 1 - step
            r = slice(ci * C_CHUNK, (ci + 1) * C_CHUNK)
            sbs[jj, d] = st_scr[jj, d].astype(BF16)
            ks[jj, d] = _dot(k_ref[r, jj * HEAD_DIM:(jj + 1) * HEAD_DIM], sbs[jj, d]).astype(BF16)
        for jj, d in chains:
            _, _, v_ref, mb_ref, _, _, gr_ref = refs[nin * d:nin * d + nin]
            ci = step if d == 0 else nchunk - 1 - step
            dc, cc = ci // 2, ci % 2
            big = slice(dc * DCHUNK, (dc + 1) * DCHUNK)
            r = slice(ci * C_CHUNK, (ci + 1) * C_CHUNK)
            place = (lambda t: jnp.concatenate([t, zeros], axis=0)) if cc == 0 else (
                lambda t: jnp.concatenate([zeros, t], axis=0))
            for e in range(2):
                u = 2 * d + e
                mb = mb_ref[jj, e, r, :]
                egc = jnp.exp(gr_ref[8 * jj + u:8 * jj + u + 1, big])
                mwn = (-(mb.astype(F32) * egc)).astype(BF16)
                vcol = slice((2 * jj + e) * HEAD_DIM, (2 * jj + e + 1) * HEAD_DIM)
                rhs = jnp.concatenate([v_ref[big, vcol], place(ks[jj, d][:, e * HEAD_DIM:(e + 1) * HEAD_DIM])], axis=0)
                vns[jj, d, e] = _dot(jnp.concatenate([mb, mwn], axis=1), rhs).astype(BF16)
        for jj, d in chains:
            q_ref, k_ref, _, _, qk_ref, _, _ = refs[nin * d:nin * d + nin]
            ci = step if d == 0 else nchunk - 1 - step
            cc = ci % 2
            r = slice(ci * C_CHUNK, (ci + 1) * C_CHUNK)
            last = (ci + 1) * C_CHUNK - 1 if d == 0 else ci * C_CHUNK
            place = (lambda t: jnp.concatenate([t, zeros], axis=0)) if cc == 0 else (
                lambda t: jnp.concatenate([zeros, t], axis=0))
            kcol = slice(jj * HEAD_DIM, (jj + 1) * HEAD_DIM)
            qf, kf = q_ref[r, kcol].astype(F32), k_ref[r, kcol].astype(F32)
            upd, gls = [], []
            for e in range(2):
                u = 2 * d + e
                vn = vns[jj, d, e]
                gcc = gc8[jj, d][r, u:u + 1]
                glast = gc8[jj, d][last:last + 1, u:u + 1]
                qd = (qf * jnp.exp(gcc)).astype(BF16)
                kd = (kf * jnp.exp(glast - gcc)).astype(BF16)
                lhs = jnp.concatenate([qd, qk_ref[jj, e, r, :]], axis=1)
                rhs = jnp.concatenate([sbs[jj, d][:, e * HEAD_DIM:(e + 1) * HEAD_DIM], place(vn)], axis=0)
                ocol = slice((2 * jj + e) * HEAD_DIM, (2 * jj + e + 1) * HEAD_DIM)
                o_refs[d][r, ocol] = _dot(lhs, rhs).astype(BF16)
                upd.append(_dot_tn(kd, vn))
                gls.append(jnp.broadcast_to(jnp.exp(glast), (1, HEAD_DIM)))
            st_scr[jj, d] = st_scr[jj, d] * jnp.concatenate(gls, axis=1) + jnp.concatenate(upd, axis=1)

    @pl.when(n == pl.num_programs(2) - 1)
    def _():
        sfin_ref[...] = st_scr[...]


def _gdn_b(qn, kn, vv, mb, qk, gcol, grow, s0, lblk=256, kb=2):
    bsz, length, kw = kn.shape
    kh = kw // HEAD_DIM
    lblk = min(lblk, length)
    nb = length // lblk

    def specs(d):
        blk = (lambda n: nb - 1 - n) if d else (lambda n: n)
        unit = pl.BlockSpec((None, kb, 2, lblk, HEAD_DIM), lambda b, j, n: (b, j, d, blk(n), 0))
        return [pl.BlockSpec((None, lblk, kb * HEAD_DIM), lambda b, j, n: (b, blk(n), j)),
                pl.BlockSpec((None, lblk, kb * HEAD_DIM), lambda b, j, n: (b, blk(n), j)),
                pl.BlockSpec((None, lblk, 2 * kb * HEAD_DIM), lambda b, j, n: (b, blk(n), j)),
                unit, unit,
                pl.BlockSpec((None, lblk, gcol.shape[2]), lambda b, j, n: (b, blk(n), 0)),
                pl.BlockSpec((None, 8 * kb, lblk), lambda b, j, n: (b, j, blk(n)))]

    st_spec = pl.BlockSpec((None, kb, 2, HEAD_DIM, 2 * HEAD_DIM), lambda b, j, n: (b, j, 0, 0, 0))
    o_shape = jax.ShapeDtypeStruct((bsz, length, 2 * kw), BF16)
    args = [qn, kn, vv, mb, qk, gcol, grow]
    return pl.pallas_call(
        _gdn_b_kernel,
        grid=(bsz, kh // kb, nb),
        in_specs=specs(0) + specs(1) + [st_spec],
        out_specs=[pl.BlockSpec((None, lblk, 2 * kb * HEAD_DIM), lambda b, j, n: (b, n, j)),
                   pl.BlockSpec((None, lblk, 2 * kb * HEAD_DIM), lambda b, j, n: (b, nb - 1 - n, j)),
                   st_spec],
        out_shape=[o_shape, o_shape, jax.ShapeDtypeStruct(s0.shape, F32)],
        scratch_shapes=[pltpu.VMEM((kb, 2, HEAD_DIM, 2 * HEAD_DIM), F32)],
        compiler_params=_params(("parallel", "parallel", "arbitrary")),
        name="gdn_scan",
    )(*args, *args, s0)


def _gate_layout(hv):
    perm, src = [], []
    for j in range(hv // 2):
        for i in range(8):
            u = i % 4
            d, e = u // 2, u % 2
            perm.append((0 if i < 4 else 2 * hv) + d * hv + 2 * j + e)
            src.append(d * hv + 2 * j + e if i < 4 else -1)
    return np.asarray(perm), np.asarray(src)


def kernel(x, c, ctx, c_ctx, w_ada, b_ada, norm_w, ev_w_in, ev_lb, ev_a_norm, ev_pool_w, ev_pool_scale,
           ev_w_out, od_w_in, od_conv, od_A_log, od_dt_bias, od_norm, od_w_out, ffn_w13, ffn_w2):
    bsz, seq, d = x.shape
    depth = w_ada.shape[0]
    ctx_len = ctx.shape[1]
    a_width = ev_lb.shape[2]
    heads_a = a_width // HEAD_DIM
    assert bsz + 1 <= 8

    cvec = jnp.zeros((8, d), F32).at[0].set(c_ctx).at[1:1 + bsz].set(c)
    mod = _ada(cvec, w_ada, b_ada).reshape(depth, 8, 6, d)
    lat_row = lambda b: b + 1
    ctx_row = lambda b: 0
    lb_all = jnp.cumsum(jax.nn.softmax(ev_lb.astype(F32), axis=1), axis=1)

    for layer in range(depth):
        need_ctx = layer < depth - 1
        j = layer // 2
        nw = norm_w[layer]
        w13 = ffn_w13[layer].astype(BF16)
        w2 = ffn_w2[layer].astype(BF16)
        if layer % 2 == 0:
            w_in = ev_w_in[j].astype(BF16)
            lb = lb_all[:, layer]
            p_l = _proj(x, mod, layer, lat_row, nw[0], w_in)
            p_c = _proj(ctx, mod, layer, ctx_row, nw[0], w_in)
            s0 = jnp.zeros((2, bsz, heads_a, HEAD_DIM, HEAD_DIM), F32)
            oc_f, oc_b, s_ctx = _hgrn(p_c, lb, s0)
            ol_f, ol_b, _ = _hgrn(p_l, lb, s_ctx)
            x = _even_out(ol_f, ol_b, p_l, x, mod, layer, lat_row, ev_a_norm[j], ev_pool_w[j],
                          ev_pool_scale[j], ev_w_out[j], nw[1], GRID_W)
            if need_ctx:
                ctx = _even_out(oc_f, oc_b, p_c, ctx, mod, layer, ctx_row, ev_a_norm[j], ev_pool_w[j],
                                ev_pool_scale[j], ev_w_out[j], nw[1], ctx_len)
        else:
            assert not need_ctx
            hv = od_A_log.shape[2]
            kh = hv // 2
            kw, vw = kh * HEAD_DIM, hv * HEAD_DIM
            nmain = 2 * kw + 2 * vw
            w_main = od_w_in[j][:, :nmain].astype(BF16)
            perm, src = _gate_layout(hv)
            wg = _split_bf16(od_w_in[j][:, nmain:][:, perm])
            lane_param = lambda t: jnp.where(src >= 0, t.reshape(-1)[np.maximum(src, 0)], 0.0).reshape(1, -1)
            alog_lane, dtb_lane = lane_param(od_A_log[j]), lane_param(od_dt_bias[j])
            conv_w = od_conv[j]

            def mix(p, pg, s0):
                gcol, grow = _gates(pg, alog_lane, dtb_lane)
                qn = _qkv(p, 0, kw, conv_w, True, HEAD_DIM ** -0.5)
                kn = _qkv(p, kw, kw, conv_w, True, 1.0)
                vv = _qkv(p, 2 * kw, vw, conv_w, False, 1.0)
                mb, qk = _gdn_a(qn, kn, gcol, grow)
                return _gdn_b(qn, kn, vv, mb, qk, gcol, grow, s0)

            p_l, g_l = _proj(x, mod, layer, lat_row, nw[0], w_main, wg)
            p_c, g_c = _proj(ctx, mod, layer, ctx_row, nw[0], w_main, wg)
            s0 = jnp.zeros((bsz, kh, 2, HEAD_DIM, 2 * HEAD_DIM), F32)
            _, _, s_ctx = mix(p_c, g_c, s0)
            o_f, o_b, _ = mix(p_l, g_l, s_ctx)
            x = _odd_out(o_f, o_b, p_l, 2 * kw + vw, x, mod, layer, lat_row, od_norm[j], od_w_out[j], nw[1])
        x = _ffn(x, mod, layer, lat_row, nw[2], nw[3], w13, w2)
        if need_ctx:
            ctx = _ffn(ctx, mod, layer, ctx_row, nw[2], nw[3], w13, w2)
    return x
```

```python
import functools

import numpy as np
import jax
import jax.numpy as jnp
from jax import lax
from jax.experimental import pallas as pl
from jax.experimental.pallas import tpu as pltpu

F32 = jnp.float32
BF16 = jnp.bfloat16
EPS = 1e-6

HEAD_DIM = 128
GRID_W = 64
POOL_WINDOWS = (2, 4, 8, 16)
A_CHUNK = 32
C_CONV = 4
C_CHUNK = 64
DCHUNK = 2 * C_CHUNK
VMEM_LIMIT = 56 * 1024 * 1024

NT = (((1,), (1,)), ((), ()))
TN = (((0,), (0,)), ((), ()))


def _dot(a, b):
    return jnp.dot(a, b, preferred_element_type=F32)


def _dot_nt(a, b):
    return lax.dot_general(a, b, NT, preferred_element_type=F32)


def _dot_tn(a, b):
    return lax.dot_general(a, b, TN, preferred_element_type=F32)


def _silu(x):
    return x * jax.nn.sigmoid(x)


def _rms(x):
    return x * lax.rsqrt(jnp.mean(x * x, axis=-1, keepdims=True) + EPS)


def _split_bf16(x):
    hi = x.astype(BF16)
    lo = (x - hi.astype(F32)).astype(BF16)
    return hi, lo


ROW_BLOCK = 16


def _row_blocks(nrows, body):
    def step(i, carry):
        body(pl.ds(pl.multiple_of(i * ROW_BLOCK, ROW_BLOCK), ROW_BLOCK))
        return carry
    lax.fori_loop(0, nrows // ROW_BLOCK, step, 0, unroll=8)


def _norm_modulate(x_ref, nw, shift, scale, hi_ref, lo_ref=None):
    def body(rows):
        h = (_rms(x_ref[rows, :]) * nw) * (1.0 + scale) + shift
        hh = h.astype(BF16)
        hi_ref[rows, :] = hh
        if lo_ref is not None:
            lo_ref[rows, :] = (h - hh.astype(F32)).astype(BF16)
    _row_blocks(x_ref.shape[0], body)


def _gated_norm_residual(x_ref, y_ref, gate, nw, o_ref):
    def body(rows):
        o_ref[rows, :] = x_ref[rows, :] + gate * (_rms(y_ref[rows, :]) * nw)
    _row_blocks(x_ref.shape[0], body)


def _params(sem):
    return pltpu.CompilerParams(dimension_semantics=sem, vmem_limit_bytes=VMEM_LIMIT)


def _ada_kernel(c_ref, w_ref, b_ref, o_ref):
    s = _silu(c_ref[...]).astype(BF16)
    o_ref[...] = _dot(s, w_ref[...].astype(BF16)) + b_ref[...]


def _ada(cvec, w_ada, b_ada):
    depth, d, n = w_ada.shape
    tn = 1024
    return pl.pallas_call(
        _ada_kernel,
        grid=(depth, n // tn),
        in_specs=[
            pl.BlockSpec((8, d), lambda l, j: (0, 0)),
            pl.BlockSpec((None, d, tn), lambda l, j: (l, 0, j)),
            pl.BlockSpec((None, 1, tn), lambda l, j: (l, 0, j)),
        ],
        out_specs=pl.BlockSpec((None, 8, tn), lambda l, j: (l, 0, j)),
        out_shape=jax.ShapeDtypeStruct((depth, 8, n), F32),
        compiler_params=_params(("parallel", "parallel")),
        name="ada",
    )(cvec, w_ada, b_ada.reshape(depth, 1, n))


def _proj_kernel(x_ref, mod_ref, nw_ref, w_ref, *rest, has_gate):
    if has_gate:
        wgh_ref, wgl_ref, o_ref, og_ref, h_scr, hl_scr = rest
    else:
        o_ref, h_scr = rest
        hl_scr = None
    j = pl.program_id(2)

    @pl.when(j == 0)
    def _():
        _norm_modulate(x_ref, nw_ref[...], mod_ref[0:1, :], mod_ref[1:2, :], h_scr, hl_scr)
        if has_gate:
            hh, hl = h_scr[...], hl_scr[...]
            og_ref[...] = _dot(hh, wgh_ref[...]) + _dot(hl, wgh_ref[...]) + _dot(hh, wgl_ref[...])

    o_ref[...] = _dot(h_scr[...], w_ref[...]).astype(o_ref.dtype)


def _proj(x, mod, layer, row_of_batch, nw, w, wg=None, tm=1024, tn=1024):
    bsz, length, d = x.shape
    n = w.shape[1]
    tm = min(tm, length)
    has_gate = wg is not None
    in_specs = [
        pl.BlockSpec((None, tm, d), lambda b, i, j: (b, i, 0)),
        pl.BlockSpec((None, None, 6, d), lambda b, i, j: (layer, row_of_batch(b), 0, 0)),
        pl.BlockSpec((1, d), lambda b, i, j: (0, 0)),
        pl.BlockSpec((d, tn), lambda b, i, j: (0, j)),
    ]
    out_specs = [pl.BlockSpec((None, tm, tn), lambda b, i, j: (b, i, j))]
    out_shape = [jax.ShapeDtypeStruct((bsz, length, n), BF16)]
    args = [x, mod, nw.reshape(1, d), w]
    if has_gate:
        ng = wg[0].shape[1]
        in_specs += [pl.BlockSpec((d, ng), lambda b, i, j: (0, 0))] * 2
        out_specs.append(pl.BlockSpec((None, tm, ng), lambda b, i, j: (b, i, 0)))
        out_shape.append(jax.ShapeDtypeStruct((bsz, length, ng), F32))
        args += list(wg)
    res = pl.pallas_call(
        functools.partial(_proj_kernel, has_gate=has_gate),
        grid=(bsz, length // tm, n // tn),
        in_specs=in_specs,
        out_specs=out_specs,
        out_shape=out_shape,
        scratch_shapes=[pltpu.VMEM((tm, d), BF16)] * (2 if has_gate else 1),
        compiler_params=_params(("parallel", "parallel", "arbitrary")),
        name="proj",
    )(*args)
    return res if has_gate else res[0]


def _hgrn_kernel(qf_ref, ff_ref, if_ref, qb_ref, fb_ref, ib_ref, lb_ref, s0_ref, trif_ref, trib_ref,
                 of_ref, ob_ref, sfin_ref, st_scr):
    n = pl.program_id(2)
    lblk = qf_ref.shape[0]
    nchunk = lblk // A_CHUNK
    hb = qf_ref.shape[1] // HEAD_DIM

    @pl.when(n == 0)
    def _():
        st_scr[...] = s0_ref[...]

    dirs = ((qf_ref, ff_ref, if_ref, of_ref, trif_ref), (qb_ref, fb_ref, ib_ref, ob_ref, trib_ref))
    pre = {}
    for d, (q_ref, f_ref, i_ref, o_ref, tri_ref) in enumerate(dirs):
        tri = tri_ref[...]
        for hh in range(hb):
            c = slice(hh * HEAD_DIM, (hh + 1) * HEAD_DIM)
            lb = lb_ref[d][:, c]
            s = f_ref[:, c].astype(F32)
            logf = jnp.log(lb + (1.0 - lb) * jax.nn.sigmoid(s))
            k = (1.0 - lb) * jax.nn.sigmoid(-s)
            qa = _silu(q_ref[:, c].astype(F32))
            v = i_ref[:, c]
            hi, lo = _split_bf16(logf)
            b = _dot(tri, hi) + _dot(tri, lo)
            b3 = b.reshape(nchunk, A_CHUNK, HEAD_DIM)
            mid = A_CHUNK // 2 - 1 if d == 0 else A_CHUNK // 2
            last = A_CHUNK - 1 if d == 0 else 0
            bmid = jnp.broadcast_to(b3[:, mid:mid + 1, :], b3.shape).reshape(b.shape)
            blast = jnp.broadcast_to(b3[:, last:last + 1, :], b3.shape).reshape(b.shape)
            qs = (qa * jnp.exp(b - bmid)).astype(BF16)
            ks = (k * jnp.exp(bmid - b)).astype(BF16)
            scores = jnp.where(tri > 0, _dot_nt(qs, ks), 0.0)
            o_intra = _dot(scores.astype(BF16), v)
            q_in = (qa * jnp.exp(b)).astype(BF16)
            k_out = (k * jnp.exp(blast - b)).astype(BF16)
            pre[d, hh] = (o_intra, q_in, k_out, jnp.exp(blast), v)
    for step in range(nchunk):
        for d in range(2):
            ci = step if d == 0 else nchunk - 1 - step
            r = slice(ci * A_CHUNK, (ci + 1) * A_CHUNK)
            for hh in range(hb):
                o_intra, q_in, k_out, dec, v = pre[d, hh]
                st = st_scr[d, hh]
                dirs[d][3][r, hh * HEAD_DIM:(hh + 1) * HEAD_DIM] = (
                    o_intra[r] + _dot_nt(q_in[r], st.astype(BF16))).astype(BF16)
                st_scr[d, hh] = st * dec[ci * A_CHUNK:ci * A_CHUNK + 1] + _dot_tn(v[r], k_out[r])

    @pl.when(n == pl.num_programs(2) - 1)
    def _():
        sfin_ref[...] = st_scr[...]


def _chunk_tri(lblk, chunk, upper):
    t = np.arange(lblk)
    same = (t[:, None] // chunk) == (t[None, :] // chunk)
    tri = (t[None, :] >= t[:, None]) if upper else (t[None, :] <= t[:, None])
    return jnp.asarray(same & tri, BF16)


def _hgrn(p, lb, s0, lblk=256, hb=4):
    bsz, length, _ = p.shape
    heads = lb.shape[1] // HEAD_DIM
    hg = heads // hb
    lblk = min(lblk, length)
    nb = length // lblk
    blk = lambda col, rev: pl.BlockSpec(
        (None, lblk, hb * HEAD_DIM),
        (lambda b, h, n: (b, nb - 1 - n, col * hg + h)) if rev else (lambda b, h, n: (b, n, col * hg + h)))
    oblk = lambda rev: pl.BlockSpec(
        (None, lblk, hb * HEAD_DIM), (lambda b, h, n: (b, nb - 1 - n, h)) if rev else (lambda b, h, n: (b, n, h)))
    st_spec = pl.BlockSpec((2, None, hb, HEAD_DIM, HEAD_DIM), lambda b, h, n: (0, b, h, 0, 0))
    tri_spec = pl.BlockSpec((lblk, lblk), lambda b, h, n: (0, 0))
    width = heads * HEAD_DIM
    return pl.pallas_call(
        _hgrn_kernel,
        grid=(bsz, hg, nb),
        in_specs=[blk(0, False), blk(1, False), blk(3, False), blk(0, True), blk(2, True), blk(3, True),
                  pl.BlockSpec((2, None, 1, hb * HEAD_DIM), lambda b, h, n: (0, h, 0, 0)),
                  st_spec, tri_spec, tri_spec],
        out_specs=[oblk(False), oblk(True), st_spec],
        out_shape=[jax.ShapeDtypeStruct((bsz, length, width), BF16),
                   jax.ShapeDtypeStruct((bsz, length, width), BF16),
                   jax.ShapeDtypeStruct(s0.shape, F32)],
        scratch_shapes=[pltpu.VMEM((2, hb, HEAD_DIM, HEAD_DIM), F32)],
        compiler_params=_params(("parallel", "parallel", "arbitrary")),
        name="hgrn2",
    )(p, p, p, p, p, p, lb.reshape(2, hg, 1, hb * HEAD_DIM), s0,
      _chunk_tri(lblk, A_CHUNK, False), _chunk_tri(lblk, A_CHUNK, True))


def _even_out_kernel(of_ref, ob_ref, g_ref, u_ref, x_ref, mod_ref, an_ref, band_ref, icnt_ref, pw_ref,
                     ps_ref, wout_ref, nw_ref, o_ref):
    o = of_ref[...].astype(F32) + ob_ref[...].astype(F32)
    g = g_ref[...].astype(F32)
    u = u_ref[...]
    heads = o.shape[1] // HEAD_DIM
    parts = []
    for h in range(heads):
        c = slice(h * HEAD_DIM, (h + 1) * HEAD_DIM)
        parts.append((_rms(o[:, c]) * an_ref[...] * _silu(g[:, c])).astype(BF16))
    gd = u.shape[1] // len(POOL_WINDOWS)
    for gi in range(len(POOL_WINDOWS)):
        c = slice(gi * gd, (gi + 1) * gd)
        ug = u[:, c]
        dlt = _dot(band_ref[gi], ug) * icnt_ref[gi] - ug.astype(F32)
        parts.append((_dot(dlt.astype(BF16), pw_ref[gi]) * ps_ref[:, c]).astype(BF16))
    y = _dot(jnp.concatenate(parts, axis=1), wout_ref[...])
    o_ref[...] = x_ref[...] + mod_ref[2:3, :] * (_rms(y) * nw_ref[...])


def _pool_consts(tm, roww):
    t = np.arange(tm)
    row, pos = t // roww, t % roww
    bands, icnts = [], []
    for win in POOL_WINDOWS:
        lo = np.clip(pos - win // 2, 0, roww - 1)
        hi = np.clip(pos + win - 1 - win // 2, 0, roww - 1)
        m = (row[:, None] == row[None, :]) & (pos[None, :] >= lo[:, None]) & (pos[None, :] <= hi[:, None])
        bands.append(m)
        icnts.append(np.broadcast_to((1.0 / (hi - lo + 1))[:, None], (tm, 256)))
    return jnp.asarray(np.stack(bands), BF16), jnp.asarray(np.stack(icnts), F32)


def _even_out(o_f, o_b, p, x, mod, layer, row_of_batch, a_norm, pool_w, pool_scale, w_out, nw, roww):
    bsz, length, d = x.shape
    width = o_f.shape[2]
    tm = 256
    assert length % tm == 0 and tm % roww == 0
    band, icnt = _pool_consts(tm, roww)
    ng, gd = pool_w.shape[0], pool_w.shape[1]
    gcol, ucol = p.shape[2] // width - 2, p.shape[2] // width - 1
    tok = lambda w, col: pl.BlockSpec((None, tm, w), lambda b, i: (b, i, col))
    const = lambda shape: pl.BlockSpec(shape, lambda b, i: (0,) * len(shape))
    return pl.pallas_call(
        _even_out_kernel,
        grid=(bsz, length // tm),
        in_specs=[tok(width, 0), tok(width, 0), tok(width, gcol), tok(width, ucol), tok(d, 0),
                  pl.BlockSpec((None, None, 6, d), lambda b, i: (layer, row_of_batch(b), 0, 0)),
                  const((1, HEAD_DIM)), const((ng, tm, tm)), const((ng, tm, 256)), const((ng, gd, gd)),
                  const((1, width)), const((2 * width, d)), const((1, d))],
        out_specs=tok(d, 0),
        out_shape=jax.ShapeDtypeStruct(x.shape, F32),
        compiler_params=_params(("parallel", "parallel")),
        name="even_out",
    )(o_f, o_b, p, p, x, mod, a_norm.reshape(1, HEAD_DIM), band, icnt, pool_w.astype(BF16),
      pool_scale.reshape(1, width), w_out.astype(BF16), nw.reshape(1, d))


def _odd_out_kernel(of_ref, ob_ref, z_ref, x_ref, mod_ref, hn_ref, w_ref, nw_ref, o_ref, *, tk):
    y = None
    for g in range(of_ref.shape[1] // tk):
        parts = []
        for h in range(tk // HEAD_DIM):
            c = slice(g * tk + h * HEAD_DIM, g * tk + (h + 1) * HEAD_DIM)
            o = of_ref[:, c].astype(F32) + ob_ref[:, c].astype(F32)
            parts.append((_rms(o) * hn_ref[...] * _silu(z_ref[:, c].astype(F32))).astype(BF16))
        part = _dot(jnp.concatenate(parts, axis=1), w_ref[g * tk:(g + 1) * tk, :])
        y = part if y is None else y + part
    o_ref[...] = x_ref[...] + mod_ref[2:3, :] * (_rms(y) * nw_ref[...])


def _odd_out(o_f, o_b, p, zcol0, x, mod, layer, row_of_batch, head_norm, w_out, nw, tm=256, tk=1024):
    bsz, length, d = x.shape
    vw = o_f.shape[2]
    tm = min(tm, length)
    zoff = zcol0 // vw
    return pl.pallas_call(
        functools.partial(_odd_out_kernel, tk=tk),
        grid=(bsz, length // tm),
        in_specs=[pl.BlockSpec((None, tm, vw), lambda b, i: (b, i, 0)),
                  pl.BlockSpec((None, tm, vw), lambda b, i: (b, i, 0)),
                  pl.BlockSpec((None, tm, vw), lambda b, i: (b, i, zoff)),
                  pl.BlockSpec((None, tm, d), lambda b, i: (b, i, 0)),
                  pl.BlockSpec((None, None, 6, d), lambda b, i: (layer, row_of_batch(b), 0, 0)),
                  pl.BlockSpec((1, HEAD_DIM), lambda b, i: (0, 0)),
                  pl.BlockSpec((vw, d), lambda b, i: (0, 0), pipeline_mode=pl.Buffered(1)),
                  pl.BlockSpec((1, d), lambda b, i: (0, 0))],
        out_specs=pl.BlockSpec((None, tm, d), lambda b, i: (b, i, 0)),
        out_shape=jax.ShapeDtypeStruct(x.shape, F32),
        compiler_params=_params(("parallel", "parallel")),
        name="odd_out",
    )(o_f, o_b, p, x, mod, head_norm.reshape(1, HEAD_DIM), w_out.astype(BF16), nw.reshape(1, d))


def _ffn_kernel(x_ref, mod_ref, nwa_ref, nwb_ref, w1_ref, w3_ref, w2_ref, o_ref, h_scr, acc_scr):
    j = pl.program_id(2)

    @pl.when(j == 0)
    def _():
        _norm_modulate(x_ref, nwa_ref[...], mod_ref[3:4, :], mod_ref[4:5, :], h_scr)

    h = h_scr[...]
    t = (_silu(_dot(h, w1_ref[...])) * _dot(h, w3_ref[...])).astype(BF16)
    y = _dot(t, w2_ref[...])

    @pl.when(j == 0)
    def _():
        acc_scr[...] = y

    @pl.when(j > 0)
    def _():
        acc_scr[...] += y

    @pl.when(j == pl.num_programs(2) - 1)
    def _():
        _gated_norm_residual(x_ref, acc_scr, mod_ref[5:6, :], nwb_ref[...], o_ref)


def _ffn(x, mod, layer, row_of_batch, nwa, nwb, w13, w2, tm=512, th=512):
    bsz, length, d = x.shape
    hidden = w2.shape[0]
    tm = min(tm, length)
    nh = hidden // th
    return pl.pallas_call(
        _ffn_kernel,
        grid=(bsz, length // tm, nh),
        in_specs=[pl.BlockSpec((None, tm, d), lambda b, i, j: (b, i, 0)),
                  pl.BlockSpec((None, None, 6, d), lambda b, i, j: (layer, row_of_batch(b), 0, 0)),
                  pl.BlockSpec((1, d), lambda b, i, j: (0, 0)),
                  pl.BlockSpec((1, d), lambda b, i, j: (0, 0)),
                  pl.BlockSpec((d, th), lambda b, i, j: (0, j)),
                  pl.BlockSpec((d, th), lambda b, i, j: (0, nh + j)),
                  pl.BlockSpec((th, d), lambda b, i, j: (j, 0))],
        out_specs=pl.BlockSpec((None, tm, d), lambda b, i, j: (b, i, 0)),
        out_shape=jax.ShapeDtypeStruct(x.shape, F32),
        scratch_shapes=[pltpu.VMEM((tm, d), BF16), pltpu.VMEM((tm, d), F32)],
        compiler_params=_params(("parallel", "parallel", "arbitrary")),
        name="ffn",
    )(x, mod, nwa.reshape(1, d), nwb.reshape(1, d), w13, w13, w2)


def _gate_kernel(pg_ref, alog_ref, dtb_ref, trif_ref, trib_ref, gcol_ref, grow_ref):
    x = pg_ref[...]
    i8 = lax.broadcasted_iota(jnp.int32, x.shape, 1) & 7
    xa = x + dtb_ref[...]
    softplus = jnp.maximum(xa, 0.0) + jnp.log(1.0 + jnp.exp(-jnp.abs(xa)))
    g = -jnp.exp(alog_ref[...]) * softplus
    g1 = g.astype(BF16)
    r1 = g - g1.astype(F32)
    g2 = r1.astype(BF16)
    g3 = (r1 - g2.astype(F32)).astype(BF16)
    trif, trib = trif_ref[...], trib_ref[...]
    cf = _dot(trif, g1) + _dot(trif, g2) + _dot(trif, g3)
    cb = _dot(trib, g1) + _dot(trib, g2) + _dot(trib, g3)
    out = jnp.where(i8 < 2, cf, jnp.where(i8 < 4, cb, jax.nn.sigmoid(x)))
    gcol_ref[...] = out
    grow_ref[...] = out.T


def _gates(pg, alog_lane, dtb_lane, tm=256):
    bsz, length, nl = pg.shape
    tm = min(tm, length)
    const = lambda shape: pl.BlockSpec(shape, lambda b, i: (0,) * len(shape))
    return pl.pallas_call(
        _gate_kernel,
        grid=(bsz, length // tm),
        in_specs=[pl.BlockSpec((None, tm, nl), lambda b, i: (b, i, 0)), const((1, nl)), const((1, nl)),
                  const((tm, tm)), const((tm, tm))],
        out_specs=[pl.BlockSpec((None, tm, nl), lambda b, i: (b, i, 0)),
                   pl.BlockSpec((None, nl, tm), lambda b, i: (b, 0, i))],
        out_shape=[jax.ShapeDtypeStruct((bsz, length, nl), F32), jax.ShapeDtypeStruct((bsz, nl, length), F32)],
        compiler_params=_params(("parallel", "parallel")),
        name="gdn_gates",
    )(pg, alog_lane, dtb_lane, _chunk_tri(tm, C_CHUNK, False), _chunk_tri(tm, C_CHUNK, True))


def _qkv_kernel(prev_ref, x_ref, next_ref, cw_ref, shift_ref, o_ref, *, l2, scale):
    i = pl.program_id(1)
    tm = x_ref.shape[0]
    xb = x_ref[...]
    prev8 = jnp.where(i > 0, prev_ref[8:16, :].astype(F32), 0.0)
    next8 = jnp.where(i < pl.num_programs(1) - 1, next_ref[0:8, :].astype(F32), 0.0)
    sub = lax.broadcasted_iota(jnp.int32, (8, xb.shape[1]), 0)
    top8, bot8 = xb[0:8].astype(F32), xb[tm - 8:tm].astype(F32)
    acc = None
    nshift = 0
    for j in range(C_CONV):
        off = j - C_CONV // 2
        if off == 0:
            sh = xb.astype(F32)
        else:
            body = _dot(shift_ref[nshift], xb)
            nshift += 1
            if off < 0:
                fixed = jnp.where(sub < -off, pltpu.roll(prev8, -off, 0), pltpu.roll(top8, -off, 0))
                sh = jnp.concatenate([fixed, body[8:]], axis=0)
            else:
                fixed = jnp.where(sub >= 8 - off, pltpu.roll(next8, 8 - off, 0), pltpu.roll(bot8, 8 - off, 0))
                sh = jnp.concatenate([body[:tm - 8], fixed], axis=0)
        term = cw_ref[j:j + 1, :] * sh
        acc = term if acc is None else acc + term
    y = _silu(acc)
    if l2:
        parts = []
        for h in range(y.shape[1] // HEAD_DIM):
            yh = y[:, h * HEAD_DIM:(h + 1) * HEAD_DIM]
            parts.append(yh * (lax.rsqrt(jnp.sum(yh * yh, axis=-1, keepdims=True) + EPS) * scale))
        y = jnp.concatenate(parts, axis=1)
    o_ref[...] = y.astype(BF16)


def _qkv(p, col0, width, conv_w, l2, scale, tm=256, tc=1024):
    bsz, length, _ = p.shape
    tm = min(tm, length)
    c0 = col0 // tc
    halo = 16
    nsub = length // halo
    t = np.arange(tm)
    offs = [j - C_CONV // 2 for j in range(C_CONV) if j != C_CONV // 2]
    shifts = jnp.asarray(np.stack([t[None, :] == t[:, None] + off for off in offs]), BF16)
    return pl.pallas_call(
        functools.partial(_qkv_kernel, l2=l2, scale=scale),
        grid=(bsz, length // tm, width // tc),
        in_specs=[pl.BlockSpec((None, halo, tc),
                               lambda b, i, c: (b, jnp.maximum(i * (tm // halo) - 1, 0), c0 + c)),
                  pl.BlockSpec((None, tm, tc), lambda b, i, c: (b, i, c0 + c)),
                  pl.BlockSpec((None, halo, tc),
                               lambda b, i, c: (b, jnp.minimum((i + 1) * (tm // halo), nsub - 1), c0 + c)),
                  pl.BlockSpec((C_CONV, tc), lambda b, i, c: (0, c0 + c)),
                  pl.BlockSpec(shifts.shape, lambda b, i, c: (0, 0, 0))],
        out_specs=pl.BlockSpec((None, tm, tc), lambda b, i, c: (b, i, c)),
        out_shape=jax.ShapeDtypeStruct((bsz, length, width), BF16),
        compiler_params=_params(("parallel", "parallel", "parallel")),
        name="gdn_qkv",
    )(p, p, p, conv_w, shifts)


def _gdn_masks():
    t = np.arange(DCHUNK)
    same = (t[:, None] // C_CHUNK) == (t[None, :] // C_CHUNK)
    le, lt = t[None, :] <= t[:, None], t[None, :] < t[:, None]
    masks = [same & le, same & lt, same & le.T, same & lt.T]
    m = 1
    while m < C_CHUNK:
        masks.append(((t[:, None] // (2 * m)) == (t[None, :] // (2 * m))) & ((t[:, None] // m) != (t[None, :] // m)))
        m *= 2
    return jnp.asarray(np.stack(masks), F32)


def _block_diag2(t):
    z = jnp.zeros((t.shape[0], HEAD_DIM), t.dtype)
    return jnp.concatenate([jnp.concatenate([t[:, :HEAD_DIM], z], axis=1),
                            jnp.concatenate([z, t[:, HEAD_DIM:]], axis=1)], axis=0)


def _gdn_a_kernel(q_ref, k_ref, v_ref, gcol_ref, grow_ref, mask_ref, lmask_ref, u_ref, w_ref, qk_ref,
                  a_scr, x_scr):
    j = pl.program_id(1)
    shift = lax.rem(HEAD_DIM - 8 * j, HEAD_DIM)
    nlvl = lmask_ref.shape[0]
    ndc = q_ref.shape[0] // DCHUNK
    rid = lax.broadcasted_iota(jnp.int32, (DCHUNK, DCHUNK), 0)
    cid = lax.broadcasted_iota(jnp.int32, (DCHUNK, DCHUNK), 1)
    eye = (rid == cid).astype(F32)
    brows, erows = [], []
    for dc in range(ndc):
        rows = slice(dc * DCHUNK, (dc + 1) * DCHUNK)
        q, k = q_ref[rows, :], k_ref[rows, :]
        kk, qk = _dot_nt(k, k), _dot_nt(q, k)
        gc8 = pltpu.roll(gcol_ref[rows, :], shift, 1)
        gr = grow_ref[:, rows]
        for d in range(2):
            causal, strict = mask_ref[2 * d], mask_ref[2 * d + 1]
            pair_a, pair_x = [], []
            for e in range(2):
                u = 2 * d + e
                gcc, bcol = gc8[:, u:u + 1], gc8[:, 4 + u:5 + u]
                gcr = gr[u:u + 1, :]
                dec = jnp.exp(jnp.where(causal > 0, gcc - gcr, 0.0)) * causal
                a = strict * (bcol * kk * dec)
                pair_a.append(a.astype(BF16))
                pair_x.append(eye - a * mask_ref[4])
                qk_ref[u, rows, :] = (qk * dec).astype(BF16)
            a_scr[2 * dc + d] = jnp.concatenate(pair_a, axis=1)
            x_scr[2 * dc + d] = jnp.concatenate(pair_x, axis=1)
            brows.append(jnp.concatenate([gr[4 + 2 * d:5 + 2 * d, :], gr[5 + 2 * d:6 + 2 * d, :]], axis=1))
            erows.append(jnp.exp(jnp.concatenate([gr[2 * d:2 * d + 1, :], gr[2 * d + 1:2 * d + 2, :]], axis=1)))
    for lvl in range(nlvl):
        lm = lmask_ref[lvl]
        xb = [x_scr[c].astype(BF16) for c in range(2 * ndc)]
        pm = [_dot(a_scr[c] * lm, _block_diag2(xb[c])).astype(BF16) for c in range(2 * ndc)]
        for c in range(2 * ndc):
            x_scr[c] = x_scr[c] - _dot(xb[c], _block_diag2(pm[c]))
    for c in range(2 * ndc):
        dc, d = c // 2, c % 2
        rows = slice(dc * DCHUNK, (dc + 1) * DCHUNK)
        mb = x_scr[c] * brows[c]
        k = k_ref[rows, :]
        up = _dot(mb.astype(BF16), _block_diag2(v_ref[rows, :])).astype(BF16)
        wp = _dot((mb * erows[c]).astype(BF16), _block_diag2(jnp.concatenate([k, k], axis=1))).astype(BF16)
        for e in range(2):
            u_ref[2 * d + e, rows, :] = up[:, e * HEAD_DIM:(e + 1) * HEAD_DIM]
            w_ref[2 * d + e, rows, :] = wp[:, e * HEAD_DIM:(e + 1) * HEAD_DIM]


def _gdn_a(qn, kn, vv, gcol, grow, lblk=1024):
    bsz, length, kw = qn.shape
    kh = kw // HEAD_DIM
    lblk = min(lblk, length)
    masks = _gdn_masks()
    lmasks = jnp.concatenate([masks[5:], masks[5:]], axis=2).astype(BF16)
    masks = masks[:5]
    npair = 2 * (lblk // DCHUNK)
    unit = jax.ShapeDtypeStruct((bsz, kh, 4, length, HEAD_DIM), BF16)
    unit_spec = pl.BlockSpec((None, None, 4, lblk, HEAD_DIM), lambda b, j, n: (b, j, 0, n, 0))
    return pl.pallas_call(
        _gdn_a_kernel,
        grid=(bsz, kh, length // lblk),
        in_specs=[pl.BlockSpec((None, lblk, HEAD_DIM), lambda b, j, n: (b, n, j)),
                  pl.BlockSpec((None, lblk, HEAD_DIM), lambda b, j, n: (b, n, j)),
                  pl.BlockSpec((None, lblk, 2 * HEAD_DIM), lambda b, j, n: (b, n, j)),
                  pl.BlockSpec((None, lblk, HEAD_DIM), lambda b, j, n: (b, n, 0)),
                  pl.BlockSpec((None, 8, lblk), lambda b, j, n: (b, j, n)),
                  pl.BlockSpec(masks.shape, lambda b, j, n: (0, 0, 0)),
                  pl.BlockSpec(lmasks.shape, lambda b, j, n: (0, 0, 0))],
        out_specs=[unit_spec, unit_spec, unit_spec],
        out_shape=[unit, unit, unit],
        scratch_shapes=[pltpu.VMEM((npair, DCHUNK, 2 * HEAD_DIM), BF16),
                        pltpu.VMEM((npair, DCHUNK, 2 * HEAD_DIM), F32)],
        compiler_params=_params(("parallel", "parallel", "parallel")),
        name="gdn_local",
    )(qn, kn, vv, gcol, grow, masks, lmasks)


def _gdn_b_kernel(*refs):
    nin = 6
    s0_ref, of_ref, ob_ref, sfin_ref, st_scr = refs[2 * nin:]
    n = pl.program_id(2)
    kb = st_scr.shape[0]

    @pl.when(n == 0)
    def _():
        st_scr[...] = s0_ref[...]

    zeros = jnp.zeros((C_CHUNK, 2 * HEAD_DIM), BF16)
    o_refs = (of_ref, ob_ref)
    nchunk = refs[0].shape[0] // C_CHUNK
    chains = [(jj, d) for jj in range(kb) for d in range(2)]
    pair = lambda ref, jj, r: jnp.concatenate([ref[jj, 0, r, :], ref[jj, 1, r, :]], axis=1)
    gc8 = {}
    for jj, d in chains:
        shift = lax.rem(HEAD_DIM - 8 * (pl.program_id(1) * kb + jj), HEAD_DIM)
        gc8[jj, d] = pltpu.roll(refs[nin * d + 5][...], shift, 1)
    for step in range(nchunk):
        vns, qss, ekds, gls = {}, {}, {}, {}
        for jj, d in chains:
            q_ref, _, u_ref, w_ref, _, _ = refs[nin * d:nin * d + nin]
            ci = step if d == 0 else nchunk - 1 - step
            r = slice(ci * C_CHUNK, (ci + 1) * C_CHUNK)
            last = (ci + 1) * C_CHUNK - 1 if d == 0 else ci * C_CHUNK
            qf = q_ref[r, jj * HEAD_DIM:(jj + 1) * HEAD_DIM].astype(F32)
            qd, ekd, gl = [], [], []
            for e in range(2):
                gcc = gc8[jj, d][r, 2 * d + e:2 * d + e + 1]
                glast = gc8[jj, d][last:last + 1, 2 * d + e:2 * d + e + 1]
                qd.append((qf * jnp.exp(gcc)).astype(BF16))
                ekd.append(jnp.exp(glast - gcc))
                gl.append(jnp.broadcast_to(jnp.exp(glast), (1, HEAD_DIM)))
            lhs = jnp.concatenate([pair(w_ref, jj, r), jnp.concatenate(qd, axis=1)], axis=0)
            res = _dot(lhs, _block_diag2(st_scr[jj, d].astype(BF16)))
            vns[jj, d] = pair(u_ref, jj, r).astype(F32) - res[:C_CHUNK]
            qss[jj, d], ekds[jj, d], gls[jj, d] = res[C_CHUNK:], ekd, jnp.concatenate(gl, axis=1)
        for jj, d in chains:
            _, k_ref, _, _, qk_ref, _ = refs[nin * d:nin * d + nin]
            ci = step if d == 0 else nchunk - 1 - step
            r = slice(ci * C_CHUNK, (ci + 1) * C_CHUNK)
            vn = vns[jj, d]
            vnb = vn.astype(BF16)
            placed = jnp.concatenate([vnb, zeros] if ci % 2 == 0 else [zeros, vnb], axis=0)
            o = qss[jj, d] + _dot(pair(qk_ref, jj, r), _block_diag2(placed))
            o_refs[d][r, 2 * jj * HEAD_DIM:2 * (jj + 1) * HEAD_DIM] = o.astype(BF16)
            vs = jnp.concatenate([vn[:, :HEAD_DIM] * ekds[jj, d][0], vn[:, HEAD_DIM:] * ekds[jj, d][1]], axis=1)
            upd = _dot_tn(k_ref[r, jj * HEAD_DIM:(jj + 1) * HEAD_DIM], vs.astype(BF16))
            st_scr[jj, d] = st_scr[jj, d] * gls[jj, d] + upd

    @pl.when(n == pl.num_programs(2) - 1)
    def _():
        sfin_ref[...] = st_scr[...]


def _gdn_b(qn, kn, uu, ww, qk, gcol, s0, lblk=256, kb=4):
    bsz, length, kw = kn.shape
    kh = kw // HEAD_DIM
    lblk = min(lblk, length)
    nb = length // lblk

    def specs(d):
        blk = (lambda n: nb - 1 - n) if d else (lambda n: n)
        unit = pl.BlockSpec((None, kb, 2, lblk, HEAD_DIM), lambda b, j, n: (b, j, d, blk(n), 0))
        return [pl.BlockSpec((None, lblk, kb * HEAD_DIM), lambda b, j, n: (b, blk(n), j)),
                pl.BlockSpec((None, lblk, kb * HEAD_DIM), lambda b, j, n: (b, blk(n), j)),
                unit, unit, unit,
                pl.BlockSpec((None, lblk, gcol.shape[2]), lambda b, j, n: (b, blk(n), 0))]

    st_spec = pl.BlockSpec((None, kb, 2, HEAD_DIM, 2 * HEAD_DIM), lambda b, j, n: (b, j, 0, 0, 0))
    o_shape = jax.ShapeDtypeStruct((bsz, length, 2 * kw), BF16)
    args = [qn, kn, uu, ww, qk, gcol]
    return pl.pallas_call(
        _gdn_b_kernel,
        grid=(bsz, kh // kb, nb),
        in_specs=specs(0) + specs(1) + [st_spec],
        out_specs=[pl.BlockSpec((None, lblk, 2 * kb * HEAD_DIM), lambda b, j, n: (b, n, j)),
                   pl.BlockSpec((None, lblk, 2 * kb * HEAD_DIM), lambda b, j, n: (b, nb - 1 - n, j)),
                   st_spec],
        out_shape=[o_shape, o_shape, jax.ShapeDtypeStruct(s0.shape, F32)],
        scratch_shapes=[pltpu.VMEM((kb, 2, HEAD_DIM, 2 * HEAD_DIM), F32)],
        compiler_params=_params(("parallel", "parallel", "arbitrary")),
        name="gdn_scan",
    )(*args, *args, s0)


def _gate_layout(hv):
    perm, src = [], []
    for j in range(hv // 2):
        for i in range(8):
            u = i % 4
            d, e = u // 2, u % 2
            perm.append((0 if i < 4 else 2 * hv) + d * hv + 2 * j + e)
            src.append(d * hv + 2 * j + e if i < 4 else -1)
    return np.asarray(perm), np.asarray(src)


def kernel(x, c, ctx, c_ctx, w_ada, b_ada, norm_w, ev_w_in, ev_lb, ev_a_norm, ev_pool_w, ev_pool_scale,
           ev_w_out, od_w_in, od_conv, od_A_log, od_dt_bias, od_norm, od_w_out, ffn_w13, ffn_w2):
    bsz, seq, d = x.shape
    depth = w_ada.shape[0]
    ctx_len = ctx.shape[1]
    a_width = ev_lb.shape[2]
    heads_a = a_width // HEAD_DIM
    assert bsz + 1 <= 8

    cvec = jnp.zeros((8, d), F32).at[0].set(c_ctx).at[1:1 + bsz].set(c)
    mod = _ada(cvec, w_ada, b_ada).reshape(depth, 8, 6, d)
    lat_row = lambda b: b + 1
    ctx_row = lambda b: 0
    lb_all = jnp.cumsum(jax.nn.softmax(ev_lb.astype(F32), axis=1), axis=1)

    for layer in range(depth):
        need_ctx = layer < depth - 1
        j = layer // 2
        nw = norm_w[layer]
        w13 = ffn_w13[layer].astype(BF16)
        w2 = ffn_w2[layer].astype(BF16)
        if layer % 2 == 0:
            w_in = ev_w_in[j].astype(BF16)
            lb = lb_all[:, layer]
            p_l = _proj(x, mod, layer, lat_row, nw[0], w_in)
            p_c = _proj(ctx, mod, layer, ctx_row, nw[0], w_in)
            s0 = jnp.zeros((2, bsz, heads_a, HEAD_DIM, HEAD_DIM), F32)
            oc_f, oc_b, s_ctx = _hgrn(p_c, lb, s0)
            ol_f, ol_b, _ = _hgrn(p_l, lb, s_ctx)
            x = _even_out(ol_f, ol_b, p_l, x, mod, layer, lat_row, ev_a_norm[j], ev_pool_w[j],
                          ev_pool_scale[j], ev_w_out[j], nw[1], GRID_W)
            if need_ctx:
                ctx = _even_out(oc_f, oc_b, p_c, ctx, mod, layer, ctx_row, ev_a_norm[j], ev_pool_w[j],
                                ev_pool_scale[j], ev_w_out[j], nw[1], ctx_len)
        else:
            assert not need_ctx
            hv = od_A_log.shape[2]
            kh = hv // 2
            kw, vw = kh * HEAD_DIM, hv * HEAD_DIM
            nmain = 2 * kw + 2 * vw
            w_main = od_w_in[j][:, :nmain].astype(BF16)
            perm, src = _gate_layout(hv)
            wg = _split_bf16(od_w_in[j][:, nmain:][:, perm])
            lane_param = lambda t: jnp.where(src >= 0, t.reshape(-1)[np.maximum(src, 0)], 0.0).reshape(1, -1)
            alog_lane, dtb_lane = lane_param(od_A_log[j]), lane_param(od_dt_bias[j])
            conv_w = od_conv[j]

            def mix(p, pg, s0):
                gcol, grow = _gates(pg, alog_lane, dtb_lane)
                qn = _qkv(p, 0, kw, conv_w, True, HEAD_DIM ** -0.5)
                kn = _qkv(p, kw, kw, conv_w, True, 1.0)
                vv = _qkv(p, 2 * kw, vw, conv_w, False, 1.0)
                uu, ww, qk = _gdn_a(qn, kn, vv, gcol, grow)
                return _gdn_b(qn, kn, uu, ww, qk, gcol, s0)

            p_l, g_l = _proj(x, mod, layer, lat_row, nw[0], w_main, wg)
            p_c, g_c = _proj(ctx, mod, layer, ctx_row, nw[0], w_main, wg)
            s0 = jnp.zeros((bsz, kh, 2, HEAD_DIM, 2 * HEAD_DIM), F32)
            _, _, s_ctx = mix(p_c, g_c, s0)
            o_f, o_b, _ = mix(p_l, g_l, s_ctx)
            x = _odd_out(o_f, o_b, p_l, 2 * kw + vw, x, mod, layer, lat_row, od_norm[j], od_w_out[j], nw[1])
        x = _ffn(x, mod, layer, lat_row, nw[2], nw[3], w13, w2)
        if need_ctx:
            ctx = _ffn(ctx, mod, layer, ctx_row, nw[2], nw[3], w13, w2)
    return x
```

```python
import functools

import numpy as np
import jax
import jax.numpy as jnp
from jax import lax
from jax.experimental import pallas as pl
from jax.experimental.pallas import tpu as pltpu

F32 = jnp.float32
BF16 = jnp.bfloat16
EPS = 1e-6

HEAD_DIM = 128
GRID_W = 64
POOL_WINDOWS = (2, 4, 8, 16)
A_CHUNK = 32
C_CONV = 4
C_CHUNK = 64
DCHUNK = 2 * C_CHUNK
VMEM_LIMIT = 56 * 1024 * 1024

NT = (((1,), (1,)), ((), ()))
TN = (((0,), (0,)), ((), ()))


def _dot(a, b):
    return jnp.dot(a, b, preferred_element_type=F32)


def _dot_nt(a, b):
    return lax.dot_general(a, b, NT, preferred_element_type=F32)


def _dot_tn(a, b):
    return lax.dot_general(a, b, TN, preferred_element_type=F32)


def _silu(x):
    return x * jax.nn.sigmoid(x)


def _rms(x):
    return x * lax.rsqrt(jnp.mean(x * x, axis=-1, keepdims=True) + EPS)


def _split_bf16(x):
    hi = x.astype(BF16)
    lo = (x - hi.astype(F32)).astype(BF16)
    return hi, lo


ROW_BLOCK = 16


def _row_blocks(nrows, body):
    def step(i, carry):
        body(pl.ds(pl.multiple_of(i * ROW_BLOCK, ROW_BLOCK), ROW_BLOCK))
        return carry
    lax.fori_loop(0, nrows // ROW_BLOCK, step, 0, unroll=8)


def _norm_modulate(x_ref, nw, shift, scale, hi_ref, lo_ref=None):
    def body(rows):
        h = (_rms(x_ref[rows, :]) * nw) * (1.0 + scale) + shift
        hh = h.astype(BF16)
        hi_ref[rows, :] = hh
        if lo_ref is not None:
            lo_ref[rows, :] = (h - hh.astype(F32)).astype(BF16)
    _row_blocks(x_ref.shape[0], body)


def _gated_norm_residual(x_ref, y_ref, gate, nw, o_ref):
    def body(rows):
        o_ref[rows, :] = x_ref[rows, :] + gate * (_rms(y_ref[rows, :]) * nw)
    _row_blocks(x_ref.shape[0], body)


def _params(sem):
    return pltpu.CompilerParams(dimension_semantics=sem, vmem_limit_bytes=VMEM_LIMIT)


def _ada_kernel(c_ref, w_ref, b_ref, o_ref):
    s = _silu(c_ref[...]).astype(BF16)
    o_ref[...] = _dot(s, w_ref[...].astype(BF16)) + b_ref[...]


def _ada(cvec, w_ada, b_ada):
    depth, d, n = w_ada.shape
    tn = 1024
    return pl.pallas_call(
        _ada_kernel,
        grid=(depth, n // tn),
        in_specs=[
            pl.BlockSpec((8, d), lambda l, j: (0, 0)),
            pl.BlockSpec((None, d, tn), lambda l, j: (l, 0, j)),
            pl.BlockSpec((None, 1, tn), lambda l, j: (l, 0, j)),
        ],
        out_specs=pl.BlockSpec((None, 8, tn), lambda l, j: (l, 0, j)),
        out_shape=jax.ShapeDtypeStruct((depth, 8, n), F32),
        compiler_params=_params(("parallel", "parallel")),
        name="ada",
    )(cvec, w_ada, b_ada.reshape(depth, 1, n))


def _proj_kernel(x_ref, mod_ref, nw_ref, w_ref, *rest, has_gate):
    if has_gate:
        wgh_ref, wgl_ref, o_ref, og_ref, h_scr, hl_scr = rest
    else:
        o_ref, h_scr = rest
        hl_scr = None
    j = pl.program_id(2)

    @pl.when(j == 0)
    def _():
        _norm_modulate(x_ref, nw_ref[...], mod_ref[0:1, :], mod_ref[1:2, :], h_scr, hl_scr)
        if has_gate:
            hh, hl = h_scr[...], hl_scr[...]
            og_ref[...] = _dot(hh, wgh_ref[...]) + _dot(hl, wgh_ref[...]) + _dot(hh, wgl_ref[...])

    o_ref[...] = _dot(h_scr[...], w_ref[...]).astype(o_ref.dtype)


def _proj(x, mod, layer, row_of_batch, nw, ws, wl, n, wg=None, tm=1024, tn=1024):
    bsz, length, d = x.shape
    tm = min(tm, length)
    has_gate = wg is not None
    in_specs = [
        pl.BlockSpec((None, tm, d), lambda b, i, j: (b, i, 0)),
        pl.BlockSpec((None, None, 6, d), lambda b, i, j: (layer, row_of_batch(b), 0, 0)),
        pl.BlockSpec((1, d), lambda b, i, j: (0, 0)),
        pl.BlockSpec((None, d, tn), lambda b, i, j: (wl, 0, j)),
    ]
    out_specs = [pl.BlockSpec((None, tm, tn), lambda b, i, j: (b, i, j))]
    out_shape = [jax.ShapeDtypeStruct((bsz, length, n), BF16)]
    args = [x, mod, nw.reshape(1, d), ws]
    if has_gate:
        ng = wg[0].shape[1]
        in_specs += [pl.BlockSpec((d, ng), lambda b, i, j: (0, 0))] * 2
        out_specs.append(pl.BlockSpec((None, tm, ng), lambda b, i, j: (b, i, 0)))
        out_shape.append(jax.ShapeDtypeStruct((bsz, length, ng), F32))
        args += list(wg)
    res = pl.pallas_call(
        functools.partial(_proj_kernel, has_gate=has_gate),
        grid=(bsz, length // tm, n // tn),
        in_specs=in_specs,
        out_specs=out_specs,
        out_shape=out_shape,
        scratch_shapes=[pltpu.VMEM((tm, d), BF16)] * (2 if has_gate else 1),
        compiler_params=_params(("parallel", "parallel", "arbitrary")),
        name="proj",
    )(*args)
    return res if has_gate else res[0]


def _hgrn_kernel(qf_ref, ff_ref, if_ref, qb_ref, fb_ref, ib_ref, lb_ref, s0_ref, trif_ref, trib_ref,
                 of_ref, ob_ref, sfin_ref, st_scr):
    n = pl.program_id(2)
    lblk = qf_ref.shape[0]
    nchunk = lblk // A_CHUNK
    hb = qf_ref.shape[1] // HEAD_DIM

    @pl.when(n == 0)
    def _():
        st_scr[...] = s0_ref[...]

    dirs = ((qf_ref, ff_ref, if_ref, of_ref, trif_ref), (qb_ref, fb_ref, ib_ref, ob_ref, trib_ref))
    pre = {}
    for d, (q_ref, f_ref, i_ref, o_ref, tri_ref) in enumerate(dirs):
        tri = tri_ref[...]
        for hh in range(hb):
            c = slice(hh * HEAD_DIM, (hh + 1) * HEAD_DIM)
            lb = lb_ref[d][:, c]
            s = f_ref[:, c].astype(F32)
            logf = jnp.log(lb + (1.0 - lb) * jax.nn.sigmoid(s))
            k = (1.0 - lb) * jax.nn.sigmoid(-s)
            qa = _silu(q_ref[:, c].astype(F32))
            v = i_ref[:, c]
            hi, lo = _split_bf16(logf)
            b = _dot(tri, hi) + _dot(tri, lo)
            b3 = b.reshape(nchunk, A_CHUNK, HEAD_DIM)
            mid = A_CHUNK // 2 - 1 if d == 0 else A_CHUNK // 2
            last = A_CHUNK - 1 if d == 0 else 0
            bmid = jnp.broadcast_to(b3[:, mid:mid + 1, :], b3.shape).reshape(b.shape)
            blast = jnp.broadcast_to(b3[:, last:last + 1, :], b3.shape).reshape(b.shape)
            qs = (qa * jnp.exp(b - bmid)).astype(BF16)
            ks = (k * jnp.exp(bmid - b)).astype(BF16)
            scores = jnp.where(tri > 0, _dot_nt(qs, ks), 0.0)
            o_intra = _dot(scores.astype(BF16), v)
            q_in = (qa * jnp.exp(b)).astype(BF16)
            k_out = (k * jnp.exp(blast - b)).astype(BF16)
            pre[d, hh] = (o_intra, q_in, k_out, jnp.exp(blast), v)
    for step in range(nchunk):
        for d in range(2):
            ci = step if d == 0 else nchunk - 1 - step
            r = slice(ci * A_CHUNK, (ci + 1) * A_CHUNK)
            for hh in range(hb):
                o_intra, q_in, k_out, dec, v = pre[d, hh]
                st = st_scr[d, hh]
                dirs[d][3][r, hh * HEAD_DIM:(hh + 1) * HEAD_DIM] = (
                    o_intra[r] + _dot_nt(q_in[r], st.astype(BF16))).astype(BF16)
                st_scr[d, hh] = st * dec[ci * A_CHUNK:ci * A_CHUNK + 1] + _dot_tn(v[r], k_out[r])

    @pl.when(n == pl.num_programs(2) - 1)
    def _():
        sfin_ref[...] = st_scr[...]


def _chunk_tri(lblk, chunk, upper):
    t = np.arange(lblk)
    same = (t[:, None] // chunk) == (t[None, :] // chunk)
    tri = (t[None, :] >= t[:, None]) if upper else (t[None, :] <= t[:, None])
    return jnp.asarray(same & tri, BF16)


def _hgrn(p, lb, s0, lblk=256, hb=4):
    bsz, length, _ = p.shape
    heads = lb.shape[1] // HEAD_DIM
    hg = heads // hb
    lblk = min(lblk, length)
    nb = length // lblk
    blk = lambda col, rev: pl.BlockSpec(
        (None, lblk, hb * HEAD_DIM),
        (lambda b, h, n: (b, nb - 1 - n, col * hg + h)) if rev else (lambda b, h, n: (b, n, col * hg + h)))
    oblk = lambda rev: pl.BlockSpec(
        (None, lblk, hb * HEAD_DIM), (lambda b, h, n: (b, nb - 1 - n, h)) if rev else (lambda b, h, n: (b, n, h)))
    st_spec = pl.BlockSpec((2, None, hb, HEAD_DIM, HEAD_DIM), lambda b, h, n: (0, b, h, 0, 0))
    tri_spec = pl.BlockSpec((lblk, lblk), lambda b, h, n: (0, 0))
    width = heads * HEAD_DIM
    return pl.pallas_call(
        _hgrn_kernel,
        grid=(bsz, hg, nb),
        in_specs=[blk(0, False), blk(1, False), blk(3, False), blk(0, True), blk(2, True), blk(3, True),
                  pl.BlockSpec((2, None, 1, hb * HEAD_DIM), lambda b, h, n: (0, h, 0, 0)),
                  st_spec, tri_spec, tri_spec],
        out_specs=[oblk(False), oblk(True), st_spec],
        out_shape=[jax.ShapeDtypeStruct((bsz, length, width), BF16),
                   jax.ShapeDtypeStruct((bsz, length, width), BF16),
                   jax.ShapeDtypeStruct(s0.shape, F32)],
        scratch_shapes=[pltpu.VMEM((2, hb, HEAD_DIM, HEAD_DIM), F32)],
        compiler_params=_params(("parallel", "parallel", "arbitrary")),
        name="hgrn2",
    )(p, p, p, p, p, p, lb.reshape(2, hg, 1, hb * HEAD_DIM), s0,
      _chunk_tri(lblk, A_CHUNK, False), _chunk_tri(lblk, A_CHUNK, True))


def _even_out_kernel(of_ref, ob_ref, g_ref, u_ref, x_ref, mod_ref, an_ref, band_ref, icnt_ref, pw_ref,
                     ps_ref, wout_ref, nw_ref, o_ref):
    o = of_ref[...].astype(F32) + ob_ref[...].astype(F32)
    g = g_ref[...].astype(F32)
    u = u_ref[...]
    heads = o.shape[1] // HEAD_DIM
    parts = []
    for h in range(heads):
        c = slice(h * HEAD_DIM, (h + 1) * HEAD_DIM)
        parts.append((_rms(o[:, c]) * an_ref[...] * _silu(g[:, c])).astype(BF16))
    gd = u.shape[1] // len(POOL_WINDOWS)
    for gi in range(len(POOL_WINDOWS)):
        c = slice(gi * gd, (gi + 1) * gd)
        ug = u[:, c]
        dlt = _dot(band_ref[gi], ug) * icnt_ref[gi] - ug.astype(F32)
        parts.append((_dot(dlt.astype(BF16), pw_ref[gi]) * ps_ref[:, c]).astype(BF16))
    y = _dot(jnp.concatenate(parts, axis=1), wout_ref[...])
    o_ref[...] = x_ref[...] + mod_ref[2:3, :] * (_rms(y) * nw_ref[...])


def _pool_consts(tm, roww):
    t = np.arange(tm)
    row, pos = t // roww, t % roww
    bands, icnts = [], []
    for win in POOL_WINDOWS:
        lo = np.clip(pos - win // 2, 0, roww - 1)
        hi = np.clip(pos + win - 1 - win // 2, 0, roww - 1)
        m = (row[:, None] == row[None, :]) & (pos[None, :] >= lo[:, None]) & (pos[None, :] <= hi[:, None])
        bands.append(m)
        icnts.append(np.broadcast_to((1.0 / (hi - lo + 1))[:, None], (tm, 256)))
    return jnp.asarray(np.stack(bands), BF16), jnp.asarray(np.stack(icnts), F32)


def _even_out(o_f, o_b, p, x, mod, layer, row_of_batch, a_norm, pool_w, pool_scale, w_outs, wl, nw, roww):
    bsz, length, d = x.shape
    width = o_f.shape[2]
    tm = 256
    assert length % tm == 0 and tm % roww == 0
    band, icnt = _pool_consts(tm, roww)
    ng, gd = pool_w.shape[0], pool_w.shape[1]
    gcol, ucol = p.shape[2] // width - 2, p.shape[2] // width - 1
    tok = lambda w, col: pl.BlockSpec((None, tm, w), lambda b, i: (b, i, col))
    const = lambda shape: pl.BlockSpec(shape, lambda b, i: (0,) * len(shape))
    return pl.pallas_call(
        _even_out_kernel,
        grid=(bsz, length // tm),
        in_specs=[tok(width, 0), tok(width, 0), tok(width, gcol), tok(width, ucol), tok(d, 0),
                  pl.BlockSpec((None, None, 6, d), lambda b, i: (layer, row_of_batch(b), 0, 0)),
                  const((1, HEAD_DIM)), const((ng, tm, tm)), const((ng, tm, 256)), const((ng, gd, gd)),
                  const((1, width)), pl.BlockSpec((None, 2 * width, d), lambda b, i: (wl, 0, 0)), const((1, d))],
        out_specs=tok(d, 0),
        out_shape=jax.ShapeDtypeStruct(x.shape, F32),
        compiler_params=_params(("parallel", "parallel")),
        name="even_out",
    )(o_f, o_b, p, p, x, mod, a_norm.reshape(1, HEAD_DIM), band, icnt, pool_w.astype(BF16),
      pool_scale.reshape(1, width), w_outs, nw.reshape(1, d))


def _odd_out_kernel(of_ref, ob_ref, z_ref, x_ref, mod_ref, hn_ref, w_ref, nw_ref, o_ref, *, tk):
    y = None
    for g in range(of_ref.shape[1] // tk):
        parts = []
        for h in range(tk // HEAD_DIM):
            c = slice(g * tk + h * HEAD_DIM, g * tk + (h + 1) * HEAD_DIM)
            o = of_ref[:, c].astype(F32) + ob_ref[:, c].astype(F32)
            parts.append((_rms(o) * hn_ref[...] * _silu(z_ref[:, c].astype(F32))).astype(BF16))
        part = _dot(jnp.concatenate(parts, axis=1), w_ref[g * tk:(g + 1) * tk, :])
        y = part if y is None else y + part
    o_ref[...] = x_ref[...] + mod_ref[2:3, :] * (_rms(y) * nw_ref[...])


def _odd_out(o_f, o_b, p, zcol0, x, mod, layer, row_of_batch, head_norm, w_outs, wl, nw, tm=256, tk=1024):
    bsz, length, d = x.shape
    vw = o_f.shape[2]
    tm = min(tm, length)
    zoff = zcol0 // vw
    return pl.pallas_call(
        functools.partial(_odd_out_kernel, tk=tk),
        grid=(bsz, length // tm),
        in_specs=[pl.BlockSpec((None, tm, vw), lambda b, i: (b, i, 0)),
                  pl.BlockSpec((None, tm, vw), lambda b, i: (b, i, 0)),
                  pl.BlockSpec((None, tm, vw), lambda b, i: (b, i, zoff)),
                  pl.BlockSpec((None, tm, d), lambda b, i: (b, i, 0)),
                  pl.BlockSpec((None, None, 6, d), lambda b, i: (layer, row_of_batch(b), 0, 0)),
                  pl.BlockSpec((1, HEAD_DIM), lambda b, i: (0, 0)),
                  pl.BlockSpec((None, vw, d), lambda b, i: (wl, 0, 0), pipeline_mode=pl.Buffered(1)),
                  pl.BlockSpec((1, d), lambda b, i: (0, 0))],
        out_specs=pl.BlockSpec((None, tm, d), lambda b, i: (b, i, 0)),
        out_shape=jax.ShapeDtypeStruct(x.shape, F32),
        compiler_params=_params(("parallel", "parallel")),
        name="odd_out",
    )(o_f, o_b, p, x, mod, head_norm.reshape(1, HEAD_DIM), w_outs, nw.reshape(1, d))


def _ffn_kernel(x_ref, mod_ref, nwa_ref, nwb_ref, w1_ref, w3_ref, w2_ref, o_ref, h_scr, acc_scr):
    j = pl.program_id(2)

    @pl.when(j == 0)
    def _():
        _norm_modulate(x_ref, nwa_ref[...], mod_ref[3:4, :], mod_ref[4:5, :], h_scr)

    h = h_scr[...]
    t = (_silu(_dot(h, w1_ref[...])) * _dot(h, w3_ref[...])).astype(BF16)
    y = _dot(t, w2_ref[...])

    @pl.when(j == 0)
    def _():
        acc_scr[...] = y

    @pl.when(j > 0)
    def _():
        acc_scr[...] += y

    @pl.when(j == pl.num_programs(2) - 1)
    def _():
        _gated_norm_residual(x_ref, acc_scr, mod_ref[5:6, :], nwb_ref[...], o_ref)


def _ffn(x, mod, layer, row_of_batch, nwa, nwb, w13s, w2s, tm=512, th=512):
    bsz, length, d = x.shape
    hidden = w2s.shape[1]
    tm = min(tm, length)
    nh = hidden // th
    return pl.pallas_call(
        _ffn_kernel,
        grid=(bsz, length // tm, nh),
        in_specs=[pl.BlockSpec((None, tm, d), lambda b, i, j: (b, i, 0)),
                  pl.BlockSpec((None, None, 6, d), lambda b, i, j: (layer, row_of_batch(b), 0, 0)),
                  pl.BlockSpec((1, d), lambda b, i, j: (0, 0)),
                  pl.BlockSpec((1, d), lambda b, i, j: (0, 0)),
                  pl.BlockSpec((None, d, th), lambda b, i, j: (layer, 0, j)),
                  pl.BlockSpec((None, d, th), lambda b, i, j: (layer, 0, nh + j)),
                  pl.BlockSpec((None, th, d), lambda b, i, j: (layer, j, 0))],
        out_specs=pl.BlockSpec((None, tm, d), lambda b, i, j: (b, i, 0)),
        out_shape=jax.ShapeDtypeStruct(x.shape, F32),
        scratch_shapes=[pltpu.VMEM((tm, d), BF16), pltpu.VMEM((tm, d), F32)],
        compiler_params=_params(("parallel", "parallel", "arbitrary")),
        name="ffn",
    )(x, mod, nwa.reshape(1, d), nwb.reshape(1, d), w13s, w13s, w2s)


def _gate_kernel(pg_ref, alog_ref, dtb_ref, trif_ref, trib_ref, gcol_ref, grow_ref):
    x = pg_ref[...]
    lane = lax.broadcasted_iota(jnp.int32, x.shape, 1)
    hv = x.shape[1] // 4
    xa = x + dtb_ref[...]
    softplus = jnp.maximum(xa, 0.0) + jnp.log(1.0 + jnp.exp(-jnp.abs(xa)))
    g = -jnp.exp(alog_ref[...]) * softplus
    g1 = g.astype(BF16)
    r1 = g - g1.astype(F32)
    g2 = r1.astype(BF16)
    g3 = (r1 - g2.astype(F32)).astype(BF16)
    trif, trib = trif_ref[...], trib_ref[...]
    cf = _dot(trif, g1) + _dot(trif, g2) + _dot(trif, g3)
    cb = _dot(trib, g1) + _dot(trib, g2) + _dot(trib, g3)
    out = jnp.where(lane < hv, cf, jnp.where(lane < 2 * hv, cb, jax.nn.sigmoid(x)))
    gcol_ref[...] = out
    grow_ref[...] = out.T


def _gates(pg, alog_lane, dtb_lane, tm=256):
    bsz, length, nl = pg.shape
    tm = min(tm, length)
    const = lambda shape: pl.BlockSpec(shape, lambda b, i: (0,) * len(shape))
    return pl.pallas_call(
        _gate_kernel,
        grid=(bsz, length // tm),
        in_specs=[pl.BlockSpec((None, tm, nl), lambda b, i: (b, i, 0)), const((1, nl)), const((1, nl)),
                  const((tm, tm)), const((tm, tm))],
        out_specs=[pl.BlockSpec((None, tm, nl), lambda b, i: (b, i, 0)),
                   pl.BlockSpec((None, nl, tm), lambda b, i: (b, 0, i))],
        out_shape=[jax.ShapeDtypeStruct((bsz, length, nl), F32), jax.ShapeDtypeStruct((bsz, nl, length), F32)],
        compiler_params=_params(("parallel", "parallel")),
        name="gdn_gates",
    )(pg, alog_lane, dtb_lane, _chunk_tri(tm, C_CHUNK, False), _chunk_tri(tm, C_CHUNK, True))


def _qkv_kernel(prev_ref, x_ref, next_ref, cw_ref, shift_ref, o_ref, *, l2, scale):
    i = pl.program_id(1)
    tm = x_ref.shape[0]
    xb = x_ref[...]
    prev8 = jnp.where(i > 0, prev_ref[8:16, :].astype(F32), 0.0)
    next8 = jnp.where(i < pl.num_programs(1) - 1, next_ref[0:8, :].astype(F32), 0.0)
    sub = lax.broadcasted_iota(jnp.int32, (8, xb.shape[1]), 0)
    top8, bot8 = xb[0:8].astype(F32), xb[tm - 8:tm].astype(F32)
    acc = None
    nshift = 0
    for j in range(C_CONV):
        off = j - C_CONV // 2
        if off == 0:
            sh = xb.astype(F32)
        else:
            body = _dot(shift_ref[nshift], xb)
            nshift += 1
            if off < 0:
                fixed = jnp.where(sub < -off, pltpu.roll(prev8, -off, 0), pltpu.roll(top8, -off, 0))
                sh = jnp.concatenate([fixed, body[8:]], axis=0)
            else:
                fixed = jnp.where(sub >= 8 - off, pltpu.roll(next8, 8 - off, 0), pltpu.roll(bot8, 8 - off, 0))
                sh = jnp.concatenate([body[:tm - 8], fixed], axis=0)
        term = cw_ref[j:j + 1, :] * sh
        acc = term if acc is None else acc + term
    y = _silu(acc)
    if l2:
        parts = []
        for h in range(y.shape[1] // HEAD_DIM):
            yh = y[:, h * HEAD_DIM:(h + 1) * HEAD_DIM]
            parts.append(yh * (lax.rsqrt(jnp.sum(yh * yh, axis=-1, keepdims=True) + EPS) * scale))
        y = jnp.concatenate(parts, axis=1)
    o_ref[...] = y.astype(BF16)


def _qkv(p, col0, width, conv_w, l2, scale, tm=256, tc=2048):
    bsz, length, _ = p.shape
    tm = min(tm, length)
    c0 = col0 // tc
    halo = 16
    nsub = length // halo
    t = np.arange(tm)
    offs = [j - C_CONV // 2 for j in range(C_CONV) if j != C_CONV // 2]
    shifts = jnp.asarray(np.stack([t[None, :] == t[:, None] + off for off in offs]), BF16)
    return pl.pallas_call(
        functools.partial(_qkv_kernel, l2=l2, scale=scale),
        grid=(bsz, length // tm, width // tc),
        in_specs=[pl.BlockSpec((None, halo, tc),
                               lambda b, i, c: (b, jnp.maximum(i * (tm // halo) - 1, 0), c0 + c)),
                  pl.BlockSpec((None, tm, tc), lambda b, i, c: (b, i, c0 + c)),
                  pl.BlockSpec((None, halo, tc),
                               lambda b, i, c: (b, jnp.minimum((i + 1) * (tm // halo), nsub - 1), c0 + c)),
                  pl.BlockSpec((C_CONV, tc), lambda b, i, c: (0, c0 + c)),
                  pl.BlockSpec(shifts.shape, lambda b, i, c: (0, 0, 0))],
        out_specs=pl.BlockSpec((None, tm, tc), lambda b, i, c: (b, i, c)),
        out_shape=jax.ShapeDtypeStruct((bsz, length, width), BF16),
        compiler_params=_params(("parallel", "parallel", "parallel")),
        name="gdn_qkv",
    )(p, p, p, conv_w, shifts)


def _gdn_masks():
    t = np.arange(DCHUNK)
    same = (t[:, None] // C_CHUNK) == (t[None, :] // C_CHUNK)
    le, lt = t[None, :] <= t[:, None], t[None, :] < t[:, None]
    masks = [same & le, same & lt, same & le.T, same & lt.T]
    m = 1
    while m < C_CHUNK:
        masks.append(((t[:, None] // (2 * m)) == (t[None, :] // (2 * m))) & ((t[:, None] // m) != (t[None, :] // m)))
        m *= 2
    return jnp.asarray(np.stack(masks), F32)


def _block_diag2(t):
    z = jnp.zeros((t.shape[0], HEAD_DIM), t.dtype)
    return jnp.concatenate([jnp.concatenate([t[:, :HEAD_DIM], z], axis=1),
                            jnp.concatenate([z, t[:, HEAD_DIM:]], axis=1)], axis=0)


def _pair_dot(lhs, rhs):
    h = HEAD_DIM
    return jnp.concatenate([_dot(lhs[:, :h], rhs[:, :h]), _dot(lhs[:, h:], rhs[:, h:])], axis=1)


def _gdn_a_kernel(q_ref, k_ref, v_ref, gcol_ref, grow_ref, mask_ref, lmask_ref, u_ref, w_ref, qk_ref,
                  a_scr, x_scr):
    j = pl.program_id(1)
    hv = gcol_ref.shape[1] // 4
    shift = lax.rem(HEAD_DIM - 2 * j, HEAD_DIM)
    nlvl = lmask_ref.shape[0]
    ndc = q_ref.shape[0] // DCHUNK
    rid = lax.broadcasted_iota(jnp.int32, (DCHUNK, DCHUNK), 0)
    cid = lax.broadcasted_iota(jnp.int32, (DCHUNK, DCHUNK), 1)
    eye = (rid == cid).astype(F32)
    sub8 = lax.broadcasted_iota(jnp.int32, (8, DCHUNK), 0)
    brows, erows = [], []
    for dc in range(ndc):
        rows = slice(dc * DCHUNK, (dc + 1) * DCHUNK)
        q, k = q_ref[rows, :], k_ref[rows, :]
        kk, qk = _dot_nt(k, k), _dot_nt(q, k)
        gc8 = pltpu.roll(gcol_ref[rows, :], shift, 1)

        def gr(lane0):
            g8 = grow_ref[pl.ds(pl.multiple_of(lane0 // 8 * 8 + 8 * (j // 4), 8), 8), rows]
            return jnp.sum(jnp.where(sub8 == lane0 % 8 + 2 * lax.rem(j, 4), g8, 0.0), axis=0, keepdims=True)
        for d in range(2):
            causal, strict = mask_ref[2 * d], mask_ref[2 * d + 1]
            pair_a, pair_x = [], []
            for e in range(2):
                u = 2 * d + e
                gcc = gc8[:, d * hv + e:d * hv + e + 1]
                bcol = gc8[:, (2 + d) * hv + e:(2 + d) * hv + e + 1]
                gcr = gr(d * hv + e)
                dec = jnp.exp(jnp.where(causal > 0, gcc - gcr, 0.0)) * causal
                a = strict * (bcol * kk * dec)
                pair_a.append(a.astype(BF16))
                pair_x.append(eye - a * mask_ref[4])
                qk_ref[u, rows, :] = (qk * dec).astype(BF16)
            a_scr[2 * dc + d] = jnp.concatenate(pair_a, axis=1)
            x_scr[2 * dc + d] = jnp.concatenate(pair_x, axis=1)
            brows.append(jnp.concatenate([gr((2 + d) * hv), gr((2 + d) * hv + 1)], axis=1))
            erows.append(jnp.exp(jnp.concatenate([gr(d * hv), gr(d * hv + 1)], axis=1)))
    for lvl in range(nlvl):
        lm = lmask_ref[lvl]
        xb = [x_scr[c].astype(BF16) for c in range(2 * ndc)]
        pm = [_pair_dot(a_scr[c] * lm, xb[c]).astype(BF16) for c in range(2 * ndc)]
        for c in range(2 * ndc):
            x_scr[c] = x_scr[c] - _pair_dot(xb[c], pm[c])
    for c in range(2 * ndc):
        dc, d = c // 2, c % 2
        rows = slice(dc * DCHUNK, (dc + 1) * DCHUNK)
        mb = x_scr[c] * brows[c]
        k = k_ref[rows, :]
        up = _dot(mb.astype(BF16), _block_diag2(v_ref[rows, :])).astype(BF16)
        wp = _dot((mb * erows[c]).astype(BF16), _block_diag2(jnp.concatenate([k, k], axis=1))).astype(BF16)
        for e in range(2):
            u_ref[2 * d + e, rows, :] = up[:, e * HEAD_DIM:(e + 1) * HEAD_DIM]
            w_ref[2 * d + e, rows, :] = wp[:, e * HEAD_DIM:(e + 1) * HEAD_DIM]


def _gdn_a(qn, kn, vv, gcol, grow, lblk=1024):
    bsz, length, kw = qn.shape
    kh = kw // HEAD_DIM
    lblk = min(lblk, length)
    masks = _gdn_masks()
    lmasks = jnp.concatenate([masks[5:], masks[5:]], axis=2).astype(BF16)
    masks = masks[:5]
    npair = 2 * (lblk // DCHUNK)
    unit = jax.ShapeDtypeStruct((bsz, kh, 4, length, HEAD_DIM), BF16)
    unit_spec = pl.BlockSpec((None, None, 4, lblk, HEAD_DIM), lambda b, j, n: (b, j, 0, n, 0))
    return pl.pallas_call(
        _gdn_a_kernel,
        grid=(bsz, kh, length // lblk),
        in_specs=[pl.BlockSpec((None, lblk, HEAD_DIM), lambda b, j, n: (b, n, j)),
                  pl.BlockSpec((None, lblk, HEAD_DIM), lambda b, j, n: (b, n, j)),
                  pl.BlockSpec((None, lblk, 2 * HEAD_DIM), lambda b, j, n: (b, n, j)),
                  pl.BlockSpec((None, lblk, HEAD_DIM), lambda b, j, n: (b, n, 0)),
                  pl.BlockSpec((None, grow.shape[1], lblk), lambda b, j, n: (b, 0, n)),
                  pl.BlockSpec(masks.shape, lambda b, j, n: (0, 0, 0)),
                  pl.BlockSpec(lmasks.shape, lambda b, j, n: (0, 0, 0))],
        out_specs=[unit_spec, unit_spec, unit_spec],
        out_shape=[unit, unit, unit],
        scratch_shapes=[pltpu.VMEM((npair, DCHUNK, 2 * HEAD_DIM), BF16),
                        pltpu.VMEM((npair, DCHUNK, 2 * HEAD_DIM), F32)],
        compiler_params=_params(("parallel", "parallel", "parallel")),
        name="gdn_local",
    )(qn, kn, vv, gcol, grow, masks, lmasks)


def _gdn_b_kernel(*refs):
    nin = 6
    s0_ref, of_ref, ob_ref, sfin_ref, st_scr = refs[2 * nin:]
    n = pl.program_id(2)
    kb = st_scr.shape[0]

    @pl.when(n == 0)
    def _():
        st_scr[...] = s0_ref[...]

    zeros = jnp.zeros((C_CHUNK, 2 * HEAD_DIM), BF16)
    o_refs = (of_ref, ob_ref)
    nchunk = refs[0].shape[0] // C_CHUNK
    chains = [(jj, d) for jj in range(kb) for d in range(2)]
    pair = lambda ref, jj, r: jnp.concatenate([ref[jj, 0, r, :], ref[jj, 1, r, :]], axis=1)
    gc8 = {}
    hv = refs[5].shape[1] // 4
    for jj, d in chains:
        shift = lax.rem(HEAD_DIM - 2 * (pl.program_id(1) * kb + jj), HEAD_DIM)
        gc8[jj, d] = pltpu.roll(refs[nin * d + 5][...], shift, 1)
    for step in range(nchunk):
        vns, qss, ekds, gls = {}, {}, {}, {}
        for jj, d in chains:
            q_ref, _, u_ref, w_ref, _, _ = refs[nin * d:nin * d + nin]
            ci = step if d == 0 else nchunk - 1 - step
            r = slice(ci * C_CHUNK, (ci + 1) * C_CHUNK)
            last = (ci + 1) * C_CHUNK - 1 if d == 0 else ci * C_CHUNK
            qf = q_ref[r, jj * HEAD_DIM:(jj + 1) * HEAD_DIM].astype(F32)
            qd, ekd, gl = [], [], []
            for e in range(2):
                gcc = gc8[jj, d][r, d * hv + e:d * hv + e + 1]
                glast = gc8[jj, d][last:last + 1, d * hv + e:d * hv + e + 1]
                qd.append((qf * jnp.exp(gcc)).astype(BF16))
                ekd.append(jnp.exp(glast - gcc))
                gl.append(jnp.broadcast_to(jnp.exp(glast), (1, HEAD_DIM)))
            lhs = jnp.concatenate([pair(w_ref, jj, r), jnp.concatenate(qd, axis=1)], axis=0)
            res = _dot(lhs, _block_diag2(st_scr[jj, d].astype(BF16)))
            vns[jj, d] = pair(u_ref, jj, r).astype(F32) - res[:C_CHUNK]
            qss[jj, d], ekds[jj, d], gls[jj, d] = res[C_CHUNK:], ekd, jnp.concatenate(gl, axis=1)
        for jj, d in chains:
            _, k_ref, _, _, qk_ref, _ = refs[nin * d:nin * d + nin]
            ci = step if d == 0 else nchunk - 1 - step
            r = slice(ci * C_CHUNK, (ci + 1) * C_CHUNK)
            vn = vns[jj, d]
            vnb = vn.astype(BF16)
            placed = jnp.concatenate([vnb, zeros] if ci % 2 == 0 else [zeros, vnb], axis=0)
            o = qss[jj, d] + _dot(pair(qk_ref, jj, r), _block_diag2(placed))
            o_refs[d][r, 2 * jj * HEAD_DIM:2 * (jj + 1) * HEAD_DIM] = o.astype(BF16)
            vs = jnp.concatenate([vn[:, :HEAD_DIM] * ekds[jj, d][0], vn[:, HEAD_DIM:] * ekds[jj, d][1]], axis=1)
            upd = _dot_tn(k_ref[r, jj * HEAD_DIM:(jj + 1) * HEAD_DIM], vs.astype(BF16))
            st_scr[jj, d] = st_scr[jj, d] * gls[jj, d] + upd

    @pl.when(n == pl.num_programs(2) - 1)
    def _():
        sfin_ref[...] = st_scr[...]


def _gdn_b(qn, kn, uu, ww, qk, gcol, s0, lblk=256, kb=4):
    bsz, length, kw = kn.shape
    kh = kw // HEAD_DIM
    lblk = min(lblk, length)
    nb = length // lblk

    def specs(d):
        blk = (lambda n: nb - 1 - n) if d else (lambda n: n)
        unit = pl.BlockSpec((None, kb, 2, lblk, HEAD_DIM), lambda b, j, n: (b, j, d, blk(n), 0))
        return [pl.BlockSpec((None, lblk, kb * HEAD_DIM), lambda b, j, n: (b, blk(n), j)),
                pl.BlockSpec((None, lblk, kb * HEAD_DIM), lambda b, j, n: (b, blk(n), j)),
                unit, unit, unit,
                pl.BlockSpec((None, lblk, gcol.shape[2]), lambda b, j, n: (b, blk(n), 0))]

    st_spec = pl.BlockSpec((None, kb, 2, HEAD_DIM, 2 * HEAD_DIM), lambda b, j, n: (b, j, 0, 0, 0))
    o_shape = jax.ShapeDtypeStruct((bsz, length, 2 * kw), BF16)
    args = [qn, kn, uu, ww, qk, gcol]
    return pl.pallas_call(
        _gdn_b_kernel,
        grid=(bsz, kh // kb, nb),
        in_specs=specs(0) + specs(1) + [st_spec],
        out_specs=[pl.BlockSpec((None, lblk, 2 * kb * HEAD_DIM), lambda b, j, n: (b, n, j)),
                   pl.BlockSpec((None, lblk, 2 * kb * HEAD_DIM), lambda b, j, n: (b, nb - 1 - n, j)),
                   st_spec],
        out_shape=[o_shape, o_shape, jax.ShapeDtypeStruct(s0.shape, F32)],
        scratch_shapes=[pltpu.VMEM((kb, 2, HEAD_DIM, 2 * HEAD_DIM), F32)],
        compiler_params=_params(("parallel", "parallel", "arbitrary")),
        name="gdn_scan",
    )(*args, *args, s0)


def kernel(x, c, ctx, c_ctx, w_ada, b_ada, norm_w, ev_w_in, ev_lb, ev_a_norm, ev_pool_w, ev_pool_scale,
           ev_w_out, od_w_in, od_conv, od_A_log, od_dt_bias, od_norm, od_w_out, ffn_w13, ffn_w2):
    bsz, seq, d = x.shape
    depth = w_ada.shape[0]
    ctx_len = ctx.shape[1]
    a_width = ev_lb.shape[2]
    heads_a = a_width // HEAD_DIM
    assert bsz + 1 <= 8

    cvec = jnp.zeros((8, d), F32).at[0].set(c_ctx).at[1:1 + bsz].set(c)
    mod = _ada(cvec, w_ada, b_ada).reshape(depth, 8, 6, d)
    lat_row = lambda b: b + 1
    ctx_row = lambda b: 0
    ev_w_in_b, ev_w_out_b, od_w_in_b, od_w_out_b, ffn_w13_b, ffn_w2_b = (
        t.astype(BF16) for t in (ev_w_in, ev_w_out, od_w_in, od_w_out, ffn_w13, ffn_w2))
    lb_all = jnp.cumsum(jax.nn.softmax(ev_lb.astype(F32), axis=1), axis=1)

    for layer in range(depth):
        need_ctx = layer < depth - 1
        j = layer // 2
        nw = norm_w[layer]
        if layer % 2 == 0:
            lb = lb_all[:, layer]
            p_l = _proj(x, mod, layer, lat_row, nw[0], ev_w_in_b, j, ev_w_in.shape[2])
            p_c = _proj(ctx, mod, layer, ctx_row, nw[0], ev_w_in_b, j, ev_w_in.shape[2])
            s0 = jnp.zeros((2, bsz, heads_a, HEAD_DIM, HEAD_DIM), F32)
            oc_f, oc_b, s_ctx = _hgrn(p_c, lb, s0)
            ol_f, ol_b, _ = _hgrn(p_l, lb, s_ctx)
            x = _even_out(ol_f, ol_b, p_l, x, mod, layer, lat_row, ev_a_norm[j], ev_pool_w[j],
                          ev_pool_scale[j], ev_w_out_b, j, nw[1], GRID_W)
            if need_ctx:
                ctx = _even_out(oc_f, oc_b, p_c, ctx, mod, layer, ctx_row, ev_a_norm[j], ev_pool_w[j],
                                ev_pool_scale[j], ev_w_out_b, j, nw[1], ctx_len)
        else:
            assert not need_ctx
            hv = od_A_log.shape[2]
            kh = hv // 2
            kw, vw = kh * HEAD_DIM, hv * HEAD_DIM
            nmain = 2 * kw + 2 * vw
            wg = _split_bf16(od_w_in[j][:, nmain:])
            lane_param = lambda t: jnp.concatenate([t.reshape(1, -1), jnp.zeros((1, 2 * hv), F32)], axis=1)
            alog_lane, dtb_lane = lane_param(od_A_log[j]), lane_param(od_dt_bias[j])
            conv_w = od_conv[j]

            def mix(p, pg, s0):
                gcol, grow = _gates(pg, alog_lane, dtb_lane)
                qn = _qkv(p, 0, kw, conv_w, True, HEAD_DIM ** -0.5)
                kn = _qkv(p, kw, kw, conv_w, True, 1.0)
                vv = _qkv(p, 2 * kw, vw, conv_w, False, 1.0)
                uu, ww, qk = _gdn_a(qn, kn, vv, gcol, grow)
                return _gdn_b(qn, kn, uu, ww, qk, gcol, s0)

            p_l, g_l = _proj(x, mod, layer, lat_row, nw[0], od_w_in_b, j, nmain, wg)
            p_c, g_c = _proj(ctx, mod, layer, ctx_row, nw[0], od_w_in_b, j, nmain, wg)
            s0 = jnp.zeros((bsz, kh, 2, HEAD_DIM, 2 * HEAD_DIM), F32)
            _, _, s_ctx = mix(p_c, g_c, s0)
            o_f, o_b, _ = mix(p_l, g_l, s_ctx)
            x = _odd_out(o_f, o_b, p_l, 2 * kw + vw, x, mod, layer, lat_row, od_norm[j], od_w_out_b, j, nw[1])
        x = _ffn(x, mod, layer, lat_row, nw[2], nw[3], ffn_w13_b, ffn_w2_b)
        if need_ctx:
            ctx = _ffn(ctx, mod, layer, ctx_row, nw[2], nw[3], ffn_w13_b, ffn_w2_b)
    return x
```

```python
import functools

import numpy as np
import jax
import jax.numpy as jnp
from jax import lax
from jax.experimental import pallas as pl
from jax.experimental.pallas import tpu as pltpu

F32 = jnp.float32
BF16 = jnp.bfloat16
EPS = 1e-6

HEAD_DIM = 128
GRID_W = 64
POOL_WINDOWS = (2, 4, 8, 16)
A_CHUNK = 32
C_CONV = 4
C_CHUNK = 64
DCHUNK = 2 * C_CHUNK
VMEM_LIMIT = 56 * 1024 * 1024

NT = (((1,), (1,)), ((), ()))
TN = (((0,), (0,)), ((), ()))


def _dot(a, b):
    return jnp.dot(a, b, preferred_element_type=F32)


def _dot_nt(a, b):
    return lax.dot_general(a, b, NT, preferred_element_type=F32)


def _dot_tn(a, b):
    return lax.dot_general(a, b, TN, preferred_element_type=F32)


def _silu(x):
    return x * jax.nn.sigmoid(x)


def _rms(x):
    return x * lax.rsqrt(jnp.mean(x * x, axis=-1, keepdims=True) + EPS)


def _split_bf16(x):
    hi = x.astype(BF16)
    lo = (x - hi.astype(F32)).astype(BF16)
    return hi, lo


ROW_BLOCK = 16


def _row_blocks(nrows, body):
    def step(i, carry):
        body(pl.ds(pl.multiple_of(i * ROW_BLOCK, ROW_BLOCK), ROW_BLOCK))
        return carry
    lax.fori_loop(0, nrows // ROW_BLOCK, step, 0, unroll=8)


def _norm_modulate(x_ref, nw, shift, scale, hi_ref, lo_ref=None):
    def body(rows):
        h = (_rms(x_ref[rows, :]) * nw) * (1.0 + scale) + shift
        hh = h.astype(BF16)
        hi_ref[rows, :] = hh
        if lo_ref is not None:
            lo_ref[rows, :] = (h - hh.astype(F32)).astype(BF16)
    _row_blocks(x_ref.shape[0], body)


def _gated_norm_residual(x_ref, y_ref, gate, nw, o_ref):
    def body(rows):
        o_ref[rows, :] = x_ref[rows, :] + gate * (_rms(y_ref[rows, :]) * nw)
    _row_blocks(x_ref.shape[0], body)


def _params(sem):
    return pltpu.CompilerParams(dimension_semantics=sem, vmem_limit_bytes=VMEM_LIMIT)


def _ada_kernel(c_ref, w_ref, b_ref, o_ref):
    s = _silu(c_ref[...]).astype(BF16)
    o_ref[...] = _dot(s, w_ref[...].astype(BF16)) + b_ref[...]


def _ada(cvec, w_ada, b_ada):
    depth, d, n = w_ada.shape
    tn = 1024
    return pl.pallas_call(
        _ada_kernel,
        grid=(depth, n // tn),
        in_specs=[
            pl.BlockSpec((8, d), lambda l, j: (0, 0)),
            pl.BlockSpec((None, d, tn), lambda l, j: (l, 0, j)),
            pl.BlockSpec((None, 1, tn), lambda l, j: (l, 0, j)),
        ],
        out_specs=pl.BlockSpec((None, 8, tn), lambda l, j: (l, 0, j)),
        out_shape=jax.ShapeDtypeStruct((depth, 8, n), F32),
        compiler_params=_params(("parallel", "parallel")),
        name="ada",
    )(cvec, w_ada, b_ada.reshape(depth, 1, n))


def _proj_kernel(x_ref, mod_ref, nw_ref, w_ref, *rest, has_gate):
    if has_gate:
        wgh_ref, wgl_ref, o_ref, og_ref, h_scr, hl_scr = rest
    else:
        o_ref, h_scr = rest
        hl_scr = None
    j = pl.program_id(2)

    @pl.when(j == 0)
    def _():
        _norm_modulate(x_ref, nw_ref[...], mod_ref[0:1, :], mod_ref[1:2, :], h_scr, hl_scr)
        if has_gate:
            hh, hl = h_scr[...], hl_scr[...]
            og_ref[...] = _dot(hh, wgh_ref[...]) + _dot(hl, wgh_ref[...]) + _dot(hh, wgl_ref[...])

    o_ref[...] = _dot(h_scr[...], w_ref[...]).astype(o_ref.dtype)


def _proj(x, mod, layer, row_of_batch, nw, ws, wl, n, wg=None, tm=1024, tn=1024):
    bsz, length, d = x.shape
    tm = min(tm, length)
    has_gate = wg is not None
    in_specs = [
        pl.BlockSpec((None, tm, d), lambda b, i, j: (b, i, 0)),
        pl.BlockSpec((None, None, 6, d), lambda b, i, j: (layer, row_of_batch(b), 0, 0)),
        pl.BlockSpec((1, d), lambda b, i, j: (0, 0)),
        pl.BlockSpec((None, d, tn), lambda b, i, j: (wl, 0, j)),
    ]
    out_specs = [pl.BlockSpec((None, tm, tn), lambda b, i, j: (b, i, j))]
    out_shape = [jax.ShapeDtypeStruct((bsz, length, n), BF16)]
    args = [x, mod, nw.reshape(1, d), ws]
    if has_gate:
        ng = wg[0].shape[1]
        in_specs += [pl.BlockSpec((d, ng), lambda b, i, j: (0, 0))] * 2
        out_specs.append(pl.BlockSpec((None, tm, ng), lambda b, i, j: (b, i, 0)))
        out_shape.append(jax.ShapeDtypeStruct((bsz, length, ng), F32))
        args += list(wg)
    res = pl.pallas_call(
        functools.partial(_proj_kernel, has_gate=has_gate),
        grid=(bsz, length // tm, n // tn),
        in_specs=in_specs,
        out_specs=out_specs,
        out_shape=out_shape,
        scratch_shapes=[pltpu.VMEM((tm, d), BF16)] * (2 if has_gate else 1),
        compiler_params=_params(("parallel", "parallel", "arbitrary")),
        name="proj",
    )(*args)
    return res if has_gate else res[0]


def _hgrn_kernel(qf_ref, ff_ref, if_ref, qb_ref, fb_ref, ib_ref, lb_ref, s0_ref, trif_ref, trib_ref,
                 of_ref, ob_ref, sfin_ref, st_scr):
    n = pl.program_id(2)
    lblk = qf_ref.shape[0]
    nchunk = lblk // A_CHUNK
    hb = qf_ref.shape[1] // HEAD_DIM

    @pl.when(n == 0)
    def _():
        st_scr[...] = s0_ref[...]

    dirs = ((qf_ref, ff_ref, if_ref, of_ref, trif_ref), (qb_ref, fb_ref, ib_ref, ob_ref, trib_ref))
    pre = {}
    for d, (q_ref, f_ref, i_ref, o_ref, tri_ref) in enumerate(dirs):
        tri = tri_ref[...]
        for hh in range(hb):
            c = slice(hh * HEAD_DIM, (hh + 1) * HEAD_DIM)
            lb = lb_ref[d][:, c]
            s = f_ref[:, c].astype(F32)
            logf = jnp.log(lb + (1.0 - lb) * jax.nn.sigmoid(s))
            k = (1.0 - lb) * jax.nn.sigmoid(-s)
            qa = _silu(q_ref[:, c].astype(F32))
            v = i_ref[:, c]
            hi, lo = _split_bf16(logf)
            b = _dot(tri, hi) + _dot(tri, lo)
            b3 = b.reshape(nchunk, A_CHUNK, HEAD_DIM)
            mid = A_CHUNK // 2 - 1 if d == 0 else A_CHUNK // 2
            last = A_CHUNK - 1 if d == 0 else 0
            bmid = jnp.broadcast_to(b3[:, mid:mid + 1, :], b3.shape).reshape(b.shape)
            blast = jnp.broadcast_to(b3[:, last:last + 1, :], b3.shape).reshape(b.shape)
            qs = (qa * jnp.exp(b - bmid)).astype(BF16)
            ks = (k * jnp.exp(bmid - b)).astype(BF16)
            scores = jnp.where(tri > 0, _dot_nt(qs, ks), 0.0)
            o_intra = _dot(scores.astype(BF16), v)
            q_in = (qa * jnp.exp(b)).astype(BF16)
            k_out = (k * jnp.exp(blast - b)).astype(BF16)
            pre[d, hh] = (o_intra, q_in, k_out, jnp.exp(blast), v)
    for step in range(nchunk):
        for d in range(2):
            ci = step if d == 0 else nchunk - 1 - step
            r = slice(ci * A_CHUNK, (ci + 1) * A_CHUNK)
            for hh in range(hb):
                o_intra, q_in, k_out, dec, v = pre[d, hh]
                st = st_scr[d, hh]
                dirs[d][3][r, hh * HEAD_DIM:(hh + 1) * HEAD_DIM] = (
                    o_intra[r] + _dot_nt(q_in[r], st.astype(BF16))).astype(BF16)
                st_scr[d, hh] = st * dec[ci * A_CHUNK:ci * A_CHUNK + 1] + _dot_tn(v[r], k_out[r])

    @pl.when(n == pl.num_programs(2) - 1)
    def _():
        sfin_ref[...] = st_scr[...]


def _chunk_tri(lblk, chunk, upper):
    t = np.arange(lblk)
    same = (t[:, None] // chunk) == (t[None, :] // chunk)
    tri = (t[None, :] >= t[:, None]) if upper else (t[None, :] <= t[:, None])
    return jnp.asarray(same & tri, BF16)


def _hgrn(p, lb, s0, lblk=256, hb=4):
    bsz, length, _ = p.shape
    heads = lb.shape[1] // HEAD_DIM
    hg = heads // hb
    lblk = min(lblk, length)
    nb = length // lblk
    blk = lambda col, rev: pl.BlockSpec(
        (None, lblk, hb * HEAD_DIM),
        (lambda b, h, n: (b, nb - 1 - n, col * hg + h)) if rev else (lambda b, h, n: (b, n, col * hg + h)))
    oblk = lambda rev: pl.BlockSpec(
        (None, lblk, hb * HEAD_DIM), (lambda b, h, n: (b, nb - 1 - n, h)) if rev else (lambda b, h, n: (b, n, h)))
    st_spec = pl.BlockSpec((2, None, hb, HEAD_DIM, HEAD_DIM), lambda b, h, n: (0, b, h, 0, 0))
    tri_spec = pl.BlockSpec((lblk, lblk), lambda b, h, n: (0, 0))
    width = heads * HEAD_DIM
    return pl.pallas_call(
        _hgrn_kernel,
        grid=(bsz, hg, nb),
        in_specs=[blk(0, False), blk(1, False), blk(3, False), blk(0, True), blk(2, True), blk(3, True),
                  pl.BlockSpec((2, None, 1, hb * HEAD_DIM), lambda b, h, n: (0, h, 0, 0)),
                  st_spec, tri_spec, tri_spec],
        out_specs=[oblk(False), oblk(True), st_spec],
        out_shape=[jax.ShapeDtypeStruct((bsz, length, width), BF16),
                   jax.ShapeDtypeStruct((bsz, length, width), BF16),
                   jax.ShapeDtypeStruct(s0.shape, F32)],
        scratch_shapes=[pltpu.VMEM((2, hb, HEAD_DIM, HEAD_DIM), F32)],
        compiler_params=_params(("parallel", "parallel", "arbitrary")),
        name="hgrn2",
    )(p, p, p, p, p, p, lb.reshape(2, hg, 1, hb * HEAD_DIM), s0,
      _chunk_tri(lblk, A_CHUNK, False), _chunk_tri(lblk, A_CHUNK, True))


def _even_out_kernel(of_ref, ob_ref, g_ref, u_ref, x_ref, mod_ref, an_ref, band_ref, icnt_ref, pw_ref,
                     ps_ref, wout_ref, nw_ref, o_ref):
    o = of_ref[...].astype(F32) + ob_ref[...].astype(F32)
    g = g_ref[...].astype(F32)
    u = u_ref[...]
    heads = o.shape[1] // HEAD_DIM
    parts = []
    for h in range(heads):
        c = slice(h * HEAD_DIM, (h + 1) * HEAD_DIM)
        parts.append((_rms(o[:, c]) * an_ref[...] * _silu(g[:, c])).astype(BF16))
    gd = u.shape[1] // len(POOL_WINDOWS)
    for gi in range(len(POOL_WINDOWS)):
        c = slice(gi * gd, (gi + 1) * gd)
        ug = u[:, c]
        dlt = _dot(band_ref[gi], ug) * icnt_ref[gi] - ug.astype(F32)
        parts.append((_dot(dlt.astype(BF16), pw_ref[gi]) * ps_ref[:, c]).astype(BF16))
    y = _dot(jnp.concatenate(parts, axis=1), wout_ref[...])
    o_ref[...] = x_ref[...] + mod_ref[2:3, :] * (_rms(y) * nw_ref[...])


def _pool_consts(tm, roww):
    t = np.arange(tm)
    row, pos = t // roww, t % roww
    bands, icnts = [], []
    for win in POOL_WINDOWS:
        lo = np.clip(pos - win // 2, 0, roww - 1)
        hi = np.clip(pos + win - 1 - win // 2, 0, roww - 1)
        m = (row[:, None] == row[None, :]) & (pos[None, :] >= lo[:, None]) & (pos[None, :] <= hi[:, None])
        bands.append(m)
        icnts.append(np.broadcast_to((1.0 / (hi - lo + 1))[:, None], (tm, 256)))
    return jnp.asarray(np.stack(bands), BF16), jnp.asarray(np.stack(icnts), F32)


def _even_out(o_f, o_b, p, x, mod, layer, row_of_batch, a_norm, pool_w, pool_scale, w_outs, wl, nw, roww):
    bsz, length, d = x.shape
    width = o_f.shape[2]
    tm = 256
    assert length % tm == 0 and tm % roww == 0
    band, icnt = _pool_consts(tm, roww)
    ng, gd = pool_w.shape[0], pool_w.shape[1]
    gcol, ucol = p.shape[2] // width - 2, p.shape[2] // width - 1
    tok = lambda w, col: pl.BlockSpec((None, tm, w), lambda b, i: (b, i, col))
    const = lambda shape: pl.BlockSpec(shape, lambda b, i: (0,) * len(shape))
    return pl.pallas_call(
        _even_out_kernel,
        grid=(bsz, length // tm),
        in_specs=[tok(width, 0), tok(width, 0), tok(width, gcol), tok(width, ucol), tok(d, 0),
                  pl.BlockSpec((None, None, 6, d), lambda b, i: (layer, row_of_batch(b), 0, 0)),
                  const((1, HEAD_DIM)), const((ng, tm, tm)), const((ng, tm, 256)), const((ng, gd, gd)),
                  const((1, width)), pl.BlockSpec((None, 2 * width, d), lambda b, i: (wl, 0, 0)), const((1, d))],
        out_specs=tok(d, 0),
        out_shape=jax.ShapeDtypeStruct(x.shape, F32),
        compiler_params=_params(("parallel", "parallel")),
        name="even_out",
    )(o_f, o_b, p, p, x, mod, a_norm.reshape(1, HEAD_DIM), band, icnt, pool_w.astype(BF16),
      pool_scale.reshape(1, width), w_outs, nw.reshape(1, d))


def _odd_out_kernel(of_ref, ob_ref, z_ref, x_ref, mod_ref, hn_ref, w_ref, nw_ref, o_ref, *, tk):
    y = None
    for g in range(of_ref.shape[1] // tk):
        parts = []
        for h in range(tk // HEAD_DIM):
            c = slice(g * tk + h * HEAD_DIM, g * tk + (h + 1) * HEAD_DIM)
            o = of_ref[:, c].astype(F32) + ob_ref[:, c].astype(F32)
            parts.append((_rms(o) * hn_ref[...] * _silu(z_ref[:, c].astype(F32))).astype(BF16))
        part = _dot(jnp.concatenate(parts, axis=1), w_ref[g * tk:(g + 1) * tk, :])
        y = part if y is None else y + part
    o_ref[...] = x_ref[...] + mod_ref[2:3, :] * (_rms(y) * nw_ref[...])


def _odd_out(o_f, o_b, p, zcol0, x, mod, layer, row_of_batch, head_norm, w_outs, wl, nw, tm=256, tk=1024):
    bsz, length, d = x.shape
    vw = o_f.shape[2]
    tm = min(tm, length)
    zoff = zcol0 // vw
    return pl.pallas_call(
        functools.partial(_odd_out_kernel, tk=tk),
        grid=(bsz, length // tm),
        in_specs=[pl.BlockSpec((None, tm, vw), lambda b, i: (b, i, 0)),
                  pl.BlockSpec((None, tm, vw), lambda b, i: (b, i, 0)),
                  pl.BlockSpec((None, tm, vw), lambda b, i: (b, i, zoff)),
                  pl.BlockSpec((None, tm, d), lambda b, i: (b, i, 0)),
                  pl.BlockSpec((None, None, 6, d), lambda b, i: (layer, row_of_batch(b), 0, 0)),
                  pl.BlockSpec((1, HEAD_DIM), lambda b, i: (0, 0)),
                  pl.BlockSpec((None, vw, d), lambda b, i: (wl, 0, 0), pipeline_mode=pl.Buffered(1)),
                  pl.BlockSpec((1, d), lambda b, i: (0, 0))],
        out_specs=pl.BlockSpec((None, tm, d), lambda b, i: (b, i, 0)),
        out_shape=jax.ShapeDtypeStruct(x.shape, F32),
        compiler_params=_params(("parallel", "parallel")),
        name="odd_out",
    )(o_f, o_b, p, x, mod, head_norm.reshape(1, HEAD_DIM), w_outs, nw.reshape(1, d))


def _ffn_kernel(x_ref, mod_ref, nwa_ref, nwb_ref, w1_ref, w3_ref, w2_ref, o_ref, h_scr, acc_scr):
    j = pl.program_id(2)

    @pl.when(j == 0)
    def _():
        _norm_modulate(x_ref, nwa_ref[...], mod_ref[3:4, :], mod_ref[4:5, :], h_scr)

    h = h_scr[...]
    t = (_silu(_dot(h, w1_ref[...])) * _dot(h, w3_ref[...])).astype(BF16)
    y = _dot(t, w2_ref[...])

    @pl.when(j == 0)
    def _():
        acc_scr[...] = y

    @pl.when(j > 0)
    def _():
        acc_scr[...] += y

    @pl.when(j == pl.num_programs(2) - 1)
    def _():
        _gated_norm_residual(x_ref, acc_scr, mod_ref[5:6, :], nwb_ref[...], o_ref)


def _ffn(x, mod, layer, row_of_batch, nwa, nwb, w13s, w2s, tm=512, th=512):
    bsz, length, d = x.shape
    hidden = w2s.shape[1]
    tm = min(tm, length)
    nh = hidden // th
    return pl.pallas_call(
        _ffn_kernel,
        grid=(bsz, length // tm, nh),
        in_specs=[pl.BlockSpec((None, tm, d), lambda b, i, j: (b, i, 0)),
                  pl.BlockSpec((None, None, 6, d), lambda b, i, j: (layer, row_of_batch(b), 0, 0)),
                  pl.BlockSpec((1, d), lambda b, i, j: (0, 0)),
                  pl.BlockSpec((1, d), lambda b, i, j: (0, 0)),
                  pl.BlockSpec((None, d, th), lambda b, i, j: (layer, 0, j)),
                  pl.BlockSpec((None, d, th), lambda b, i, j: (layer, 0, nh + j)),
                  pl.BlockSpec((None, th, d), lambda b, i, j: (layer, j, 0))],
        out_specs=pl.BlockSpec((None, tm, d), lambda b, i, j: (b, i, 0)),
        out_shape=jax.ShapeDtypeStruct(x.shape, F32),
        scratch_shapes=[pltpu.VMEM((tm, d), BF16), pltpu.VMEM((tm, d), F32)],
        compiler_params=_params(("parallel", "parallel", "arbitrary")),
        name="ffn",
    )(x, mod, nwa.reshape(1, d), nwb.reshape(1, d), w13s, w13s, w2s)


def _gate_kernel(pg_ref, alog_ref, dtb_ref, trif_ref, trib_ref, gcol_ref, grow_ref):
    x = pg_ref[...]
    lane = lax.broadcasted_iota(jnp.int32, x.shape, 1)
    hv = x.shape[1] // 4
    xa = x + dtb_ref[...]
    softplus = jnp.maximum(xa, 0.0) + jnp.log(1.0 + jnp.exp(-jnp.abs(xa)))
    g = -jnp.exp(alog_ref[...]) * softplus
    g1 = g.astype(BF16)
    r1 = g - g1.astype(F32)
    g2 = r1.astype(BF16)
    g3 = (r1 - g2.astype(F32)).astype(BF16)
    trif, trib = trif_ref[...], trib_ref[...]
    cf = _dot(trif, g1) + _dot(trif, g2) + _dot(trif, g3)
    cb = _dot(trib, g1) + _dot(trib, g2) + _dot(trib, g3)
    out = jnp.where(lane < hv, cf, jnp.where(lane < 2 * hv, cb, jax.nn.sigmoid(x)))
    gcol_ref[...] = out
    grow_ref[...] = out.T


def _gates(pg, alog_lane, dtb_lane, tm=256):
    bsz, length, nl = pg.shape
    tm = min(tm, length)
    const = lambda shape: pl.BlockSpec(shape, lambda b, i: (0,) * len(shape))
    return pl.pallas_call(
        _gate_kernel,
        grid=(bsz, length // tm),
        in_specs=[pl.BlockSpec((None, tm, nl), lambda b, i: (b, i, 0)), const((1, nl)), const((1, nl)),
                  const((tm, tm)), const((tm, tm))],
        out_specs=[pl.BlockSpec((None, tm, nl), lambda b, i: (b, i, 0)),
                   pl.BlockSpec((None, nl, tm), lambda b, i: (b, 0, i))],
        out_shape=[jax.ShapeDtypeStruct((bsz, length, nl), F32), jax.ShapeDtypeStruct((bsz, nl, length), F32)],
        compiler_params=_params(("parallel", "parallel")),
        name="gdn_gates",
    )(pg, alog_lane, dtb_lane, _chunk_tri(tm, C_CHUNK, False), _chunk_tri(tm, C_CHUNK, True))


def _qkv_kernel(prev_ref, x_ref, next_ref, cw_ref, shift_ref, o_ref, *, l2, scale):
    i = pl.program_id(1)
    tm = x_ref.shape[0]
    xb = x_ref[...]
    prev8 = jnp.where(i > 0, prev_ref[8:16, :].astype(F32), 0.0)
    next8 = jnp.where(i < pl.num_programs(1) - 1, next_ref[0:8, :].astype(F32), 0.0)
    sub = lax.broadcasted_iota(jnp.int32, (8, xb.shape[1]), 0)
    top8, bot8 = xb[0:8].astype(F32), xb[tm - 8:tm].astype(F32)
    acc = None
    nshift = 0
    for j in range(C_CONV):
        off = j - C_CONV // 2
        if off == 0:
            sh = xb.astype(F32)
        else:
            body = _dot(shift_ref[nshift], xb)
            nshift += 1
            if off < 0:
                fixed = jnp.where(sub < -off, pltpu.roll(prev8, -off, 0), pltpu.roll(top8, -off, 0))
                sh = jnp.concatenate([fixed, body[8:]], axis=0)
            else:
                fixed = jnp.where(sub >= 8 - off, pltpu.roll(next8, 8 - off, 0), pltpu.roll(bot8, 8 - off, 0))
                sh = jnp.concatenate([body[:tm - 8], fixed], axis=0)
        term = cw_ref[j:j + 1, :] * sh
        acc = term if acc is None else acc + term
    y = _silu(acc)
    if l2:
        parts = []
        for h in range(y.shape[1] // HEAD_DIM):
            yh = y[:, h * HEAD_DIM:(h + 1) * HEAD_DIM]
            parts.append(yh * (lax.rsqrt(jnp.sum(yh * yh, axis=-1, keepdims=True) + EPS) * scale))
        y = jnp.concatenate(parts, axis=1)
    o_ref[...] = y.astype(BF16)


def _qkv(p, col0, width, conv_w, l2, scale, tm=256, tc=2048):
    bsz, length, _ = p.shape
    tm = min(tm, length)
    c0 = col0 // tc
    halo = 16
    nsub = length // halo
    t = np.arange(tm)
    offs = [j - C_CONV // 2 for j in range(C_CONV) if j != C_CONV // 2]
    shifts = jnp.asarray(np.stack([t[None, :] == t[:, None] + off for off in offs]), BF16)
    return pl.pallas_call(
        functools.partial(_qkv_kernel, l2=l2, scale=scale),
        grid=(bsz, length // tm, width // tc),
        in_specs=[pl.BlockSpec((None, halo, tc),
                               lambda b, i, c: (b, jnp.maximum(i * (tm // halo) - 1, 0), c0 + c)),
                  pl.BlockSpec((None, tm, tc), lambda b, i, c: (b, i, c0 + c)),
                  pl.BlockSpec((None, halo, tc),
                               lambda b, i, c: (b, jnp.minimum((i + 1) * (tm // halo), nsub - 1), c0 + c)),
                  pl.BlockSpec((C_CONV, tc), lambda b, i, c: (0, c0 + c)),
                  pl.BlockSpec(shifts.shape, lambda b, i, c: (0, 0, 0))],
        out_specs=pl.BlockSpec((None, tm, tc), lambda b, i, c: (b, i, c)),
        out_shape=jax.ShapeDtypeStruct((bsz, length, width), BF16),
        compiler_params=_params(("parallel", "parallel", "parallel")),
        name="gdn_qkv",
    )(p, p, p, conv_w, shifts)


def _gdn_masks():
    t = np.arange(DCHUNK)
    same = (t[:, None] // C_CHUNK) == (t[None, :] // C_CHUNK)
    le, lt = t[None, :] <= t[:, None], t[None, :] < t[:, None]
    masks = [same & le, same & lt, same & le.T, same & lt.T]
    m = 1
    while m < C_CHUNK:
        masks.append(((t[:, None] // (2 * m)) == (t[None, :] // (2 * m))) & ((t[:, None] // m) != (t[None, :] // m)))
        m *= 2
    return jnp.asarray(np.stack(masks), F32)


def _block_diag2(t):
    z = jnp.zeros((t.shape[0], HEAD_DIM), t.dtype)
    return jnp.concatenate([jnp.concatenate([t[:, :HEAD_DIM], z], axis=1),
                            jnp.concatenate([z, t[:, HEAD_DIM:]], axis=1)], axis=0)


def _pair_dot(lhs, rhs):
    h = HEAD_DIM
    return jnp.concatenate([_dot(lhs[:, :h], rhs[:, :h]), _dot(lhs[:, h:], rhs[:, h:])], axis=1)


def _gdn_a_kernel(q_ref, k_ref, v_ref, gcol_ref, grow_ref, mask_ref, lmask_ref, u_ref, w_ref, qk_ref,
                  a_scr, x_scr):
    j = pl.program_id(1)
    hv = gcol_ref.shape[1] // 4
    shift = lax.rem(HEAD_DIM - 2 * j, HEAD_DIM)
    nlvl = lmask_ref.shape[0]
    ndc = q_ref.shape[0] // DCHUNK
    rid = lax.broadcasted_iota(jnp.int32, (DCHUNK, DCHUNK), 0)
    cid = lax.broadcasted_iota(jnp.int32, (DCHUNK, DCHUNK), 1)
    eye = (rid == cid).astype(F32)
    sub8 = lax.broadcasted_iota(jnp.int32, (8, DCHUNK), 0)
    brows, erows = [], []
    for dc in range(ndc):
        rows = slice(dc * DCHUNK, (dc + 1) * DCHUNK)
        q, k = q_ref[rows, :], k_ref[rows, :]
        kk, qk = _dot_nt(k, k), _dot_nt(q, k)
        gc8 = pltpu.roll(gcol_ref[rows, :], shift, 1)

        def gr(lane0):
            g8 = grow_ref[pl.ds(pl.multiple_of(lane0 // 8 * 8 + 8 * (j // 4), 8), 8), rows]
            return jnp.sum(jnp.where(sub8 == lane0 % 8 + 2 * lax.rem(j, 4), g8, 0.0), axis=0, keepdims=True)
        for d in range(2):
            causal, strict = mask_ref[2 * d], mask_ref[2 * d + 1]
            pair_a, pair_x = [], []
            for e in range(2):
                u = 2 * d + e
                gcc = gc8[:, d * hv + e:d * hv + e + 1]
                bcol = gc8[:, (2 + d) * hv + e:(2 + d) * hv + e + 1]
                gcr = gr(d * hv + e)
                dec = jnp.exp(jnp.where(causal > 0, gcc - gcr, 0.0)) * causal
                a = strict * (bcol * kk * dec)
                pair_a.append(a.astype(BF16))
                pair_x.append(eye - a * mask_ref[4])
                qk_ref[u, rows, :] = (qk * dec).astype(BF16)
            a_scr[2 * dc + d] = jnp.concatenate(pair_a, axis=1)
            x_scr[2 * dc + d] = jnp.concatenate(pair_x, axis=1)
            brows.append(jnp.concatenate([gr((2 + d) * hv), gr((2 + d) * hv + 1)], axis=1))
            erows.append(jnp.exp(jnp.concatenate([gr(d * hv), gr(d * hv + 1)], axis=1)))
    for lvl in range(nlvl):
        lm = lmask_ref[lvl]
        xb = [x_scr[c].astype(BF16) for c in range(2 * ndc)]
        pm = [_pair_dot(a_scr[c] * lm, xb[c]).astype(BF16) for c in range(2 * ndc)]
        for c in range(2 * ndc):
            x_scr[c] = x_scr[c] - _pair_dot(xb[c], pm[c])
    for c in range(2 * ndc):
        dc, d = c // 2, c % 2
        rows = slice(dc * DCHUNK, (dc + 1) * DCHUNK)
        mb = x_scr[c] * brows[c]
        k = k_ref[rows, :]
        up = _dot(mb.astype(BF16), _block_diag2(v_ref[rows, :])).astype(BF16)
        wp = _dot((mb * erows[c]).astype(BF16), _block_diag2(jnp.concatenate([k, k], axis=1))).astype(BF16)
        for e in range(2):
            u_ref[2 * d + e, rows, :] = up[:, e * HEAD_DIM:(e + 1) * HEAD_DIM]
            w_ref[2 * d + e, rows, :] = wp[:, e * HEAD_DIM:(e + 1) * HEAD_DIM]


def _gdn_a(qn, kn, vv, gcol, grow, lblk=1024):
    bsz, length, kw = qn.shape
    kh = kw // HEAD_DIM
    lblk = min(lblk, length)
    masks = _gdn_masks()
    lmasks = jnp.concatenate([masks[5:], masks[5:]], axis=2).astype(BF16)
    masks = masks[:5]
    npair = 2 * (lblk // DCHUNK)
    unit = jax.ShapeDtypeStruct((bsz, kh, 4, length, HEAD_DIM), BF16)
    unit_spec = pl.BlockSpec((None, None, 4, lblk, HEAD_DIM), lambda b, j, n: (b, j, 0, n, 0))
    return pl.pallas_call(
        _gdn_a_kernel,
        grid=(bsz, kh, length // lblk),
        in_specs=[pl.BlockSpec((None, lblk, HEAD_DIM), lambda b, j, n: (b, n, j)),
                  pl.BlockSpec((None, lblk, HEAD_DIM), lambda b, j, n: (b, n, j)),
                  pl.BlockSpec((None, lblk, 2 * HEAD_DIM), lambda b, j, n: (b, n, j)),
                  pl.BlockSpec((None, lblk, HEAD_DIM), lambda b, j, n: (b, n, 0)),
                  pl.BlockSpec((None, grow.shape[1], lblk), lambda b, j, n: (b, 0, n)),
                  pl.BlockSpec(masks.shape, lambda b, j, n: (0, 0, 0)),
                  pl.BlockSpec(lmasks.shape, lambda b, j, n: (0, 0, 0))],
        out_specs=[unit_spec, unit_spec, unit_spec],
        out_shape=[unit, unit, unit],
        scratch_shapes=[pltpu.VMEM((npair, DCHUNK, 2 * HEAD_DIM), BF16),
                        pltpu.VMEM((npair, DCHUNK, 2 * HEAD_DIM), F32)],
        compiler_params=_params(("parallel", "parallel", "parallel")),
        name="gdn_local",
    )(qn, kn, vv, gcol, grow, masks, lmasks)


def _gdn_b_kernel(*refs):
    nin = 6
    s0_ref, of_ref, ob_ref, sfin_ref, st_scr = refs[2 * nin:]
    n = pl.program_id(2)
    kb = st_scr.shape[0]

    @pl.when(n == 0)
    def _():
        st_scr[...] = s0_ref[...]

    zeros = jnp.zeros((C_CHUNK, 2 * HEAD_DIM), BF16)
    o_refs = (of_ref, ob_ref)
    nchunk = refs[0].shape[0] // C_CHUNK
    chains = [(jj, d) for jj in range(kb) for d in range(2)]
    pair = lambda ref, jj, r: jnp.concatenate([ref[jj, 0, r, :], ref[jj, 1, r, :]], axis=1)
    gc8 = {}
    hv = refs[5].shape[1] // 4
    for jj, d in chains:
        shift = lax.rem(HEAD_DIM - 2 * (pl.program_id(1) * kb + jj), HEAD_DIM)
        gc8[jj, d] = pltpu.roll(refs[nin * d + 5][...], shift, 1)
    for step in range(nchunk):
        vns, qss, ekds, gls = {}, {}, {}, {}
        for jj, d in chains:
            q_ref, _, u_ref, w_ref, _, _ = refs[nin * d:nin * d + nin]
            ci = step if d == 0 else nchunk - 1 - step
            r = slice(ci * C_CHUNK, (ci + 1) * C_CHUNK)
            last = (ci + 1) * C_CHUNK - 1 if d == 0 else ci * C_CHUNK
            qf = q_ref[r, jj * HEAD_DIM:(jj + 1) * HEAD_DIM].astype(F32)
            qd, ekd, gl = [], [], []
            for e in range(2):
                gcc = gc8[jj, d][r, d * hv + e:d * hv + e + 1]
                glast = gc8[jj, d][last:last + 1, d * hv + e:d * hv + e + 1]
                qd.append((qf * jnp.exp(gcc)).astype(BF16))
                ekd.append(jnp.exp(glast - gcc))
                gl.append(jnp.broadcast_to(jnp.exp(glast), (1, HEAD_DIM)))
            lhs = jnp.concatenate([pair(w_ref, jj, r), jnp.concatenate(qd, axis=1)], axis=0)
            res = _dot(lhs, _block_diag2(st_scr[jj, d].astype(BF16)))
            vns[jj, d] = pair(u_ref, jj, r).astype(F32) - res[:C_CHUNK]
            qss[jj, d], ekds[jj, d], gls[jj, d] = res[C_CHUNK:], ekd, jnp.concatenate(gl, axis=1)
        for jj, d in chains:
            _, k_ref, _, _, qk_ref, _ = refs[nin * d:nin * d + nin]
            ci = step if d == 0 else nchunk - 1 - step
            r = slice(ci * C_CHUNK, (ci + 1) * C_CHUNK)
            vn = vns[jj, d]
            vnb = vn.astype(BF16)
            placed = jnp.concatenate([vnb, zeros] if ci % 2 == 0 else [zeros, vnb], axis=0)
            o = qss[jj, d] + _dot(pair(qk_ref, jj, r), _block_diag2(placed))
            o_refs[d][r, 2 * jj * HEAD_DIM:2 * (jj + 1) * HEAD_DIM] = o.astype(BF16)
            vs = jnp.concatenate([vn[:, :HEAD_DIM] * ekds[jj, d][0], vn[:, HEAD_DIM:] * ekds[jj, d][1]], axis=1)
            upd = _dot_tn(k_ref[r, jj * HEAD_DIM:(jj + 1) * HEAD_DIM], vs.astype(BF16))
            st_scr[jj, d] = st_scr[jj, d] * gls[jj, d] + upd

    @pl.when(n == pl.num_programs(2) - 1)
    def _():
        sfin_ref[...] = st_scr[...]


def _gdn_b(qn, kn, uu, ww, qk, gcol, s0, lblk=512, kb=4):
    bsz, length, kw = kn.shape
    kh = kw // HEAD_DIM
    lblk = min(lblk, length)
    nb = length // lblk

    def specs(d):
        blk = (lambda n: nb - 1 - n) if d else (lambda n: n)
        unit = pl.BlockSpec((None, kb, 2, lblk, HEAD_DIM), lambda b, j, n: (b, j, d, blk(n), 0))
        return [pl.BlockSpec((None, lblk, kb * HEAD_DIM), lambda b, j, n: (b, blk(n), j)),
                pl.BlockSpec((None, lblk, kb * HEAD_DIM), lambda b, j, n: (b, blk(n), j)),
                unit, unit, unit,
                pl.BlockSpec((None, lblk, gcol.shape[2]), lambda b, j, n: (b, blk(n), 0))]

    st_spec = pl.BlockSpec((None, kb, 2, HEAD_DIM, 2 * HEAD_DIM), lambda b, j, n: (b, j, 0, 0, 0))
    o_shape = jax.ShapeDtypeStruct((bsz, length, 2 * kw), BF16)
    args = [qn, kn, uu, ww, qk, gcol]
    return pl.pallas_call(
        _gdn_b_kernel,
        grid=(bsz, kh // kb, nb),
        in_specs=specs(0) + specs(1) + [st_spec],
        out_specs=[pl.BlockSpec((None, lblk, 2 * kb * HEAD_DIM), lambda b, j, n: (b, n, j)),
                   pl.BlockSpec((None, lblk, 2 * kb * HEAD_DIM), lambda b, j, n: (b, nb - 1 - n, j)),
                   st_spec],
        out_shape=[o_shape, o_shape, jax.ShapeDtypeStruct(s0.shape, F32)],
        scratch_shapes=[pltpu.VMEM((kb, 2, HEAD_DIM, 2 * HEAD_DIM), F32)],
        compiler_params=_params(("parallel", "parallel", "arbitrary")),
        name="gdn_scan",
    )(*args, *args, s0)


def kernel(x, c, ctx, c_ctx, w_ada, b_ada, norm_w, ev_w_in, ev_lb, ev_a_norm, ev_pool_w, ev_pool_scale,
           ev_w_out, od_w_in, od_conv, od_A_log, od_dt_bias, od_norm, od_w_out, ffn_w13, ffn_w2):
    bsz, seq, d = x.shape
    depth = w_ada.shape[0]
    ctx_len = ctx.shape[1]
    a_width = ev_lb.shape[2]
    heads_a = a_width // HEAD_DIM
    assert bsz + 1 <= 8

    cvec = jnp.zeros((8, d), F32).at[0].set(c_ctx).at[1:1 + bsz].set(c)
    mod = _ada(cvec, w_ada, b_ada).reshape(depth, 8, 6, d)
    lat_row = lambda b: b + 1
    ctx_row = lambda b: 0
    flat = lambda t: t.reshape(1, bsz * t.shape[1], t.shape[2])
    unflat = lambda t: t.reshape(bsz, t.shape[1] // bsz, t.shape[2])
    ev_w_in_b, ev_w_out_b, od_w_in_b, od_w_out_b, ffn_w13_b, ffn_w2_b = (
        t.astype(BF16) for t in (ev_w_in, ev_w_out, od_w_in, od_w_out, ffn_w13, ffn_w2))
    lb_all = jnp.cumsum(jax.nn.softmax(ev_lb.astype(F32), axis=1), axis=1)

    for layer in range(depth):
        need_ctx = layer < depth - 1
        j = layer // 2
        nw = norm_w[layer]
        if layer % 2 == 0:
            lb = lb_all[:, layer]
            p_l = _proj(x, mod, layer, lat_row, nw[0], ev_w_in_b, j, ev_w_in.shape[2])
            p_c = unflat(_proj(flat(ctx), mod, layer, ctx_row, nw[0], ev_w_in_b, j, ev_w_in.shape[2]))
            s0 = jnp.zeros((2, bsz, heads_a, HEAD_DIM, HEAD_DIM), F32)
            oc_f, oc_b, s_ctx = _hgrn(p_c, lb, s0)
            ol_f, ol_b, _ = _hgrn(p_l, lb, s_ctx)
            x = _even_out(ol_f, ol_b, p_l, x, mod, layer, lat_row, ev_a_norm[j], ev_pool_w[j],
                          ev_pool_scale[j], ev_w_out_b, j, nw[1], GRID_W)
            if need_ctx:
                ctx = _even_out(oc_f, oc_b, p_c, ctx, mod, layer, ctx_row, ev_a_norm[j], ev_pool_w[j],
                                ev_pool_scale[j], ev_w_out_b, j, nw[1], ctx_len)
        else:
            assert not need_ctx
            hv = od_A_log.shape[2]
            kh = hv // 2
            kw, vw = kh * HEAD_DIM, hv * HEAD_DIM
            nmain = 2 * kw + 2 * vw
            wg = _split_bf16(od_w_in[j][:, nmain:])
            lane_param = lambda t: jnp.concatenate([t.reshape(1, -1), jnp.zeros((1, 2 * hv), F32)], axis=1)
            alog_lane, dtb_lane = lane_param(od_A_log[j]), lane_param(od_dt_bias[j])
            conv_w = od_conv[j]

            def mix(p, pg, s0):
                gcol, grow = _gates(pg, alog_lane, dtb_lane)
                qn = _qkv(p, 0, kw, conv_w, True, HEAD_DIM ** -0.5)
                kn = _qkv(p, kw, kw, conv_w, True, 1.0)
                vv = _qkv(p, 2 * kw, vw, conv_w, False, 1.0)
                uu, ww, qk = _gdn_a(qn, kn, vv, gcol, grow)
                return _gdn_b(qn, kn, uu, ww, qk, gcol, s0)

            p_l, g_l = _proj(x, mod, layer, lat_row, nw[0], od_w_in_b, j, nmain, wg)
            p_c, g_c = (unflat(t) for t in _proj(flat(ctx), mod, layer, ctx_row, nw[0], od_w_in_b, j, nmain, wg))
            s0 = jnp.zeros((bsz, kh, 2, HEAD_DIM, 2 * HEAD_DIM), F32)
            _, _, s_ctx = mix(p_c, g_c, s0)
            o_f, o_b, _ = mix(p_l, g_l, s_ctx)
            x = _odd_out(o_f, o_b, p_l, 2 * kw + vw, x, mod, layer, lat_row, od_norm[j], od_w_out_b, j, nw[1])
        x = _ffn(x, mod, layer, lat_row, nw[2], nw[3], ffn_w13_b, ffn_w2_b)
        if need_ctx:
            ctx = unflat(_ffn(flat(ctx), mod, layer, ctx_row, nw[2], nw[3], ffn_w13_b, ffn_w2_b))
    return x
```

```python
import functools

import numpy as np
import jax
import jax.numpy as jnp
from jax import lax
from jax.experimental import pallas as pl
from jax.experimental.pallas import tpu as pltpu

F32 = jnp.float32
BF16 = jnp.bfloat16
EPS = 1e-6

HEAD_DIM = 128
GRID_W = 64
POOL_WINDOWS = (2, 4, 8, 16)
A_CHUNK = 32
C_CONV = 4
C_CHUNK = 64
DCHUNK = 2 * C_CHUNK
VMEM_LIMIT = 56 * 1024 * 1024

NT = (((1,), (1,)), ((), ()))
TN = (((0,), (0,)), ((), ()))


def _dot(a, b):
    return jnp.dot(a, b, preferred_element_type=F32)


def _dot_nt(a, b):
    return lax.dot_general(a, b, NT, preferred_element_type=F32)


def _dot_tn(a, b):
    return lax.dot_general(a, b, TN, preferred_element_type=F32)


def _silu(x):
    return x * jax.nn.sigmoid(x)


def _rms(x):
    return x * lax.rsqrt(jnp.mean(x * x, axis=-1, keepdims=True) + EPS)


def _split_bf16(x):
    hi = x.astype(BF16)
    lo = (x - hi.astype(F32)).astype(BF16)
    return hi, lo


ROW_BLOCK = 16


def _row_blocks(nrows, body):
    def step(i, carry):
        body(pl.ds(pl.multiple_of(i * ROW_BLOCK, ROW_BLOCK), ROW_BLOCK))
        return carry
    lax.fori_loop(0, nrows // ROW_BLOCK, step, 0, unroll=8)


def _norm_modulate(x_ref, nw, shift, scale, hi_ref, lo_ref=None):
    def body(rows):
        h = (_rms(x_ref[rows, :]) * nw) * (1.0 + scale) + shift
        hh = h.astype(BF16)
        hi_ref[rows, :] = hh
        if lo_ref is not None:
            lo_ref[rows, :] = (h - hh.astype(F32)).astype(BF16)
    _row_blocks(x_ref.shape[0], body)


def _gated_norm_residual(x_ref, y_ref, gate, nw, o_ref):
    def body(rows):
        o_ref[rows, :] = x_ref[rows, :] + gate * (_rms(y_ref[rows, :]) * nw)
    _row_blocks(x_ref.shape[0], body)


def _params(sem):
    return pltpu.CompilerParams(dimension_semantics=sem, vmem_limit_bytes=VMEM_LIMIT)


def _ada_kernel(c_ref, w_ref, b_ref, o_ref):
    s = _silu(c_ref[...]).astype(BF16)
    o_ref[...] = _dot(s, w_ref[...].astype(BF16)) + b_ref[...]


def _ada(cvec, w_ada, b_ada):
    depth, d, n = w_ada.shape
    tn = 1024
    return pl.pallas_call(
        _ada_kernel,
        grid=(depth, n // tn),
        in_specs=[
            pl.BlockSpec((8, d), lambda l, j: (0, 0)),
            pl.BlockSpec((None, d, tn), lambda l, j: (l, 0, j)),
            pl.BlockSpec((None, 1, tn), lambda l, j: (l, 0, j)),
        ],
        out_specs=pl.BlockSpec((None, 8, tn), lambda l, j: (l, 0, j)),
        out_shape=jax.ShapeDtypeStruct((depth, 8, n), F32),
        compiler_params=_params(("parallel", "parallel")),
        name="ada",
    )(cvec, w_ada, b_ada.reshape(depth, 1, n))


def _proj_kernel(x_ref, mod_ref, nw_ref, w_ref, *rest, has_gate):
    if has_gate:
        wgh_ref, wgl_ref, o_ref, og_ref, h_scr, hl_scr = rest
    else:
        o_ref, h_scr = rest
        hl_scr = None
    j = pl.program_id(2)

    @pl.when(j == 0)
    def _():
        _norm_modulate(x_ref, nw_ref[...], mod_ref[0:1, :], mod_ref[1:2, :], h_scr, hl_scr)
        if has_gate:
            hh, hl = h_scr[...], hl_scr[...]
            og_ref[...] = _dot(hh, wgh_ref[...]) + _dot(hl, wgh_ref[...]) + _dot(hh, wgl_ref[...])

    o_ref[...] = _dot(h_scr[...], w_ref[...]).astype(o_ref.dtype)


def _proj(x, mod, layer, row_of_batch, nw, ws, wl, n, wg=None, tm=1024, tn=1024):
    bsz, length, d = x.shape
    tm = min(tm, length)
    has_gate = wg is not None
    in_specs = [
        pl.BlockSpec((None, tm, d), lambda b, i, j: (b, i, 0)),
        pl.BlockSpec((None, None, 6, d), lambda b, i, j: (layer, row_of_batch(b), 0, 0)),
        pl.BlockSpec((1, d), lambda b, i, j: (0, 0)),
        pl.BlockSpec((None, d, tn), lambda b, i, j: (wl, 0, j)),
    ]
    out_specs = [pl.BlockSpec((None, tm, tn), lambda b, i, j: (b, i, j))]
    out_shape = [jax.ShapeDtypeStruct((bsz, length, n), BF16)]
    args = [x, mod, nw.reshape(1, d), ws]
    if has_gate:
        ng = wg[0].shape[1]
        in_specs += [pl.BlockSpec((d, ng), lambda b, i, j: (0, 0))] * 2
        out_specs.append(pl.BlockSpec((None, tm, ng), lambda b, i, j: (b, i, 0)))
        out_shape.append(jax.ShapeDtypeStruct((bsz, length, ng), F32))
        args += list(wg)
    res = pl.pallas_call(
        functools.partial(_proj_kernel, has_gate=has_gate),
        grid=(bsz, length // tm, n // tn),
        in_specs=in_specs,
        out_specs=out_specs,
        out_shape=out_shape,
        scratch_shapes=[pltpu.VMEM((tm, d), BF16)] * (2 if has_gate else 1),
        compiler_params=_params(("parallel", "parallel", "arbitrary")),
        name="proj",
    )(*args)
    return res if has_gate else res[0]


def _hgrn_kernel(qf_ref, ff_ref, if_ref, qb_ref, fb_ref, ib_ref, lb_ref, s0_ref, trif_ref, trib_ref,
                 of_ref, ob_ref, sfin_ref, st_scr):
    n = pl.program_id(2)
    lblk = qf_ref.shape[0]
    nchunk = lblk // A_CHUNK
    hb = qf_ref.shape[1] // HEAD_DIM

    @pl.when(n == 0)
    def _():
        st_scr[...] = s0_ref[...]

    dirs = ((qf_ref, ff_ref, if_ref, of_ref, trif_ref), (qb_ref, fb_ref, ib_ref, ob_ref, trib_ref))
    pre = {}
    for d, (q_ref, f_ref, i_ref, o_ref, tri_ref) in enumerate(dirs):
        tri = tri_ref[...]
        for hh in range(hb):
            c = slice(hh * HEAD_DIM, (hh + 1) * HEAD_DIM)
            lb = lb_ref[d][:, c]
            s = f_ref[:, c].astype(F32)
            logf = jnp.log(lb + (1.0 - lb) * jax.nn.sigmoid(s))
            k = (1.0 - lb) * jax.nn.sigmoid(-s)
            qa = _silu(q_ref[:, c].astype(F32))
            v = i_ref[:, c]
            hi, lo = _split_bf16(logf)
            b = _dot(tri, hi) + _dot(tri, lo)
            b3 = b.reshape(nchunk, A_CHUNK, HEAD_DIM)
            mid = A_CHUNK // 2 - 1 if d == 0 else A_CHUNK // 2
            last = A_CHUNK - 1 if d == 0 else 0
            bmid = jnp.broadcast_to(b3[:, mid:mid + 1, :], b3.shape).reshape(b.shape)
            blast = jnp.broadcast_to(b3[:, last:last + 1, :], b3.shape).reshape(b.shape)
            qs = (qa * jnp.exp(b - bmid)).astype(BF16)
            ks = (k * jnp.exp(bmid - b)).astype(BF16)
            scores = jnp.where(tri > 0, _dot_nt(qs, ks), 0.0)
            o_intra = _dot(scores.astype(BF16), v)
            q_in = (qa * jnp.exp(b)).astype(BF16)
            k_out = (k * jnp.exp(blast - b)).astype(BF16)
            pre[d, hh] = (o_intra, q_in, k_out, jnp.exp(blast), v)
    for step in range(nchunk):
        for d in range(2):
            ci = step if d == 0 else nchunk - 1 - step
            r = slice(ci * A_CHUNK, (ci + 1) * A_CHUNK)
            for hh in range(hb):
                o_intra, q_in, k_out, dec, v = pre[d, hh]
                st = st_scr[d, hh]
                dirs[d][3][r, hh * HEAD_DIM:(hh + 1) * HEAD_DIM] = (
                    o_intra[r] + _dot_nt(q_in[r], st.astype(BF16))).astype(BF16)
                st_scr[d, hh] = st * dec[ci * A_CHUNK:ci * A_CHUNK + 1] + _dot_tn(v[r], k_out[r])

    @pl.when(n == pl.num_programs(2) - 1)
    def _():
        sfin_ref[...] = st_scr[...]


def _chunk_tri(lblk, chunk, upper):
    t = np.arange(lblk)
    same = (t[:, None] // chunk) == (t[None, :] // chunk)
    tri = (t[None, :] >= t[:, None]) if upper else (t[None, :] <= t[:, None])
    return jnp.asarray(same & tri, BF16)


def _hgrn(p, lb, s0, lblk=256, hb=8):
    bsz, length, _ = p.shape
    heads = lb.shape[1] // HEAD_DIM
    hg = heads // hb
    lblk = min(lblk, length)
    nb = length // lblk
    blk = lambda col, rev: pl.BlockSpec(
        (None, lblk, hb * HEAD_DIM),
        (lambda b, h, n: (b, nb - 1 - n, col * hg + h)) if rev else (lambda b, h, n: (b, n, col * hg + h)))
    oblk = lambda rev: pl.BlockSpec(
        (None, lblk, hb * HEAD_DIM), (lambda b, h, n: (b, nb - 1 - n, h)) if rev else (lambda b, h, n: (b, n, h)))
    st_spec = pl.BlockSpec((2, None, hb, HEAD_DIM, HEAD_DIM), lambda b, h, n: (0, b, h, 0, 0))
    tri_spec = pl.BlockSpec((lblk, lblk), lambda b, h, n: (0, 0))
    width = heads * HEAD_DIM
    return pl.pallas_call(
        _hgrn_kernel,
        grid=(bsz, hg, nb),
        in_specs=[blk(0, False), blk(1, False), blk(3, False), blk(0, True), blk(2, True), blk(3, True),
                  pl.BlockSpec((2, None, 1, hb * HEAD_DIM), lambda b, h, n: (0, h, 0, 0)),
                  st_spec, tri_spec, tri_spec],
        out_specs=[oblk(False), oblk(True), st_spec],
        out_shape=[jax.ShapeDtypeStruct((bsz, length, width), BF16),
                   jax.ShapeDtypeStruct((bsz, length, width), BF16),
                   jax.ShapeDtypeStruct(s0.shape, F32)],
        scratch_shapes=[pltpu.VMEM((2, hb, HEAD_DIM, HEAD_DIM), F32)],
        compiler_params=_params(("parallel", "parallel", "arbitrary")),
        name="hgrn2",
    )(p, p, p, p, p, p, lb.reshape(2, hg, 1, hb * HEAD_DIM), s0,
      _chunk_tri(lblk, A_CHUNK, False), _chunk_tri(lblk, A_CHUNK, True))


def _even_out_kernel(of_ref, ob_ref, g_ref, u_ref, x_ref, mod_ref, an_ref, band_ref, icnt_ref, pw_ref,
                     ps_ref, wout_ref, nw_ref, o_ref):
    o = of_ref[...].astype(F32) + ob_ref[...].astype(F32)
    g = g_ref[...].astype(F32)
    u = u_ref[...]
    heads = o.shape[1] // HEAD_DIM
    parts = []
    for h in range(heads):
        c = slice(h * HEAD_DIM, (h + 1) * HEAD_DIM)
        parts.append((_rms(o[:, c]) * an_ref[...] * _silu(g[:, c])).astype(BF16))
    gd = u.shape[1] // len(POOL_WINDOWS)
    for gi in range(len(POOL_WINDOWS)):
        c = slice(gi * gd, (gi + 1) * gd)
        ug = u[:, c]
        dlt = _dot(band_ref[gi], ug) * icnt_ref[gi] - ug.astype(F32)
        parts.append((_dot(dlt.astype(BF16), pw_ref[gi]) * ps_ref[:, c]).astype(BF16))
    y = _dot(jnp.concatenate(parts, axis=1), wout_ref[...])
    o_ref[...] = x_ref[...] + mod_ref[2:3, :] * (_rms(y) * nw_ref[...])


def _pool_consts(tm, roww):
    t = np.arange(tm)
    row, pos = t // roww, t % roww
    bands, icnts = [], []
    for win in POOL_WINDOWS:
        lo = np.clip(pos - win // 2, 0, roww - 1)
        hi = np.clip(pos + win - 1 - win // 2, 0, roww - 1)
        m = (row[:, None] == row[None, :]) & (pos[None, :] >= lo[:, None]) & (pos[None, :] <= hi[:, None])
        bands.append(m)
        icnts.append(np.broadcast_to((1.0 / (hi - lo + 1))[:, None], (tm, 256)))
    return jnp.asarray(np.stack(bands), BF16), jnp.asarray(np.stack(icnts), F32)


def _even_out(o_f, o_b, p, x, mod, layer, row_of_batch, a_norm, pool_w, pool_scale, w_outs, wl, nw, roww):
    bsz, length, d = x.shape
    width = o_f.shape[2]
    tm = 256
    assert length % tm == 0 and tm % roww == 0
    band, icnt = _pool_consts(tm, roww)
    ng, gd = pool_w.shape[0], pool_w.shape[1]
    gcol, ucol = p.shape[2] // width - 2, p.shape[2] // width - 1
    tok = lambda w, col: pl.BlockSpec((None, tm, w), lambda b, i: (b, i, col))
    const = lambda shape: pl.BlockSpec(shape, lambda b, i: (0,) * len(shape))
    return pl.pallas_call(
        _even_out_kernel,
        grid=(bsz, length // tm),
        in_specs=[tok(width, 0), tok(width, 0), tok(width, gcol), tok(width, ucol), tok(d, 0),
                  pl.BlockSpec((None, None, 6, d), lambda b, i: (layer, row_of_batch(b), 0, 0)),
                  const((1, HEAD_DIM)), const((ng, tm, tm)), const((ng, tm, 256)), const((ng, gd, gd)),
                  const((1, width)), pl.BlockSpec((None, 2 * width, d), lambda b, i: (wl, 0, 0)), const((1, d))],
        out_specs=tok(d, 0),
        out_shape=jax.ShapeDtypeStruct(x.shape, F32),
        compiler_params=_params(("parallel", "parallel")),
        name="even_out",
    )(o_f, o_b, p, p, x, mod, a_norm.reshape(1, HEAD_DIM), band, icnt, pool_w.astype(BF16),
      pool_scale.reshape(1, width), w_outs, nw.reshape(1, d))


def _odd_out_kernel(of_ref, ob_ref, z_ref, x_ref, mod_ref, hn_ref, w_ref, nw_ref, o_ref, *, tk):
    y = None
    for g in range(of_ref.shape[1] // tk):
        parts = []
        for h in range(tk // HEAD_DIM):
            c = slice(g * tk + h * HEAD_DIM, g * tk + (h + 1) * HEAD_DIM)
            o = of_ref[:, c].astype(F32) + ob_ref[:, c].astype(F32)
            parts.append((_rms(o) * hn_ref[...] * _silu(z_ref[:, c].astype(F32))).astype(BF16))
        part = _dot(jnp.concatenate(parts, axis=1), w_ref[g * tk:(g + 1) * tk, :])
        y = part if y is None else y + part
    o_ref[...] = x_ref[...] + mod_ref[2:3, :] * (_rms(y) * nw_ref[...])


def _odd_out(o_f, o_b, p, zcol0, x, mod, layer, row_of_batch, head_norm, w_outs, wl, nw, tm=256, tk=1024):
    bsz, length, d = x.shape
    vw = o_f.shape[2]
    tm = min(tm, length)
    zoff = zcol0 // vw
    return pl.pallas_call(
        functools.partial(_odd_out_kernel, tk=tk),
        grid=(bsz, length // tm),
        in_specs=[pl.BlockSpec((None, tm, vw), lambda b, i: (b, i, 0)),
                  pl.BlockSpec((None, tm, vw), lambda b, i: (b, i, 0)),
                  pl.BlockSpec((None, tm, vw), lambda b, i: (b, i, zoff)),
                  pl.BlockSpec((None, tm, d), lambda b, i: (b, i, 0)),
                  pl.BlockSpec((None, None, 6, d), lambda b, i: (layer, row_of_batch(b), 0, 0)),
                  pl.BlockSpec((1, HEAD_DIM), lambda b, i: (0, 0)),
                  pl.BlockSpec((None, vw, d), lambda b, i: (wl, 0, 0), pipeline_mode=pl.Buffered(1)),
                  pl.BlockSpec((1, d), lambda b, i: (0, 0))],
        out_specs=pl.BlockSpec((None, tm, d), lambda b, i: (b, i, 0)),
        out_shape=jax.ShapeDtypeStruct(x.shape, F32),
        compiler_params=_params(("parallel", "parallel")),
        name="odd_out",
    )(o_f, o_b, p, x, mod, head_norm.reshape(1, HEAD_DIM), w_outs, nw.reshape(1, d))


def _ffn_kernel(x_ref, mod_ref, nwa_ref, nwb_ref, w1_ref, w3_ref, w2_ref, o_ref, h_scr, acc_scr):
    j = pl.program_id(2)

    @pl.when(j == 0)
    def _():
        _norm_modulate(x_ref, nwa_ref[...], mod_ref[3:4, :], mod_ref[4:5, :], h_scr)

    h = h_scr[...]
    t = (_silu(_dot(h, w1_ref[...])) * _dot(h, w3_ref[...])).astype(BF16)
    y = _dot(t, w2_ref[...])

    @pl.when(j == 0)
    def _():
        acc_scr[...] = y

    @pl.when(j > 0)
    def _():
        acc_scr[...] += y

    @pl.when(j == pl.num_programs(2) - 1)
    def _():
        _gated_norm_residual(x_ref, acc_scr, mod_ref[5:6, :], nwb_ref[...], o_ref)


def _ffn(x, mod, layer, row_of_batch, nwa, nwb, w13s, w2s, tm=512, th=512):
    bsz, length, d = x.shape
    hidden = w2s.shape[1]
    tm = min(tm, length)
    nh = hidden // th
    return pl.pallas_call(
        _ffn_kernel,
        grid=(bsz, length // tm, nh),
        in_specs=[pl.BlockSpec((None, tm, d), lambda b, i, j: (b, i, 0)),
                  pl.BlockSpec((None, None, 6, d), lambda b, i, j: (layer, row_of_batch(b), 0, 0)),
                  pl.BlockSpec((1, d), lambda b, i, j: (0, 0)),
                  pl.BlockSpec((1, d), lambda b, i, j: (0, 0)),
                  pl.BlockSpec((None, d, th), lambda b, i, j: (layer, 0, j)),
                  pl.BlockSpec((None, d, th), lambda b, i, j: (layer, 0, nh + j)),
                  pl.BlockSpec((None, th, d), lambda b, i, j: (layer, j, 0))],
        out_specs=pl.BlockSpec((None, tm, d), lambda b, i, j: (b, i, 0)),
        out_shape=jax.ShapeDtypeStruct(x.shape, F32),
        scratch_shapes=[pltpu.VMEM((tm, d), BF16), pltpu.VMEM((tm, d), F32)],
        compiler_params=_params(("parallel", "parallel", "arbitrary")),
        name="ffn",
    )(x, mod, nwa.reshape(1, d), nwb.reshape(1, d), w13s, w13s, w2s)


def _gate_kernel(pg_ref, alog_ref, dtb_ref, trif_ref, trib_ref, gcol_ref, grow_ref):
    x = pg_ref[...]
    lane = lax.broadcasted_iota(jnp.int32, x.shape, 1)
    hv = x.shape[1] // 4
    xa = x + dtb_ref[...]
    softplus = jnp.maximum(xa, 0.0) + jnp.log(1.0 + jnp.exp(-jnp.abs(xa)))
    g = -jnp.exp(alog_ref[...]) * softplus
    g1 = g.astype(BF16)
    r1 = g - g1.astype(F32)
    g2 = r1.astype(BF16)
    g3 = (r1 - g2.astype(F32)).astype(BF16)
    trif, trib = trif_ref[...], trib_ref[...]
    cf = _dot(trif, g1) + _dot(trif, g2) + _dot(trif, g3)
    cb = _dot(trib, g1) + _dot(trib, g2) + _dot(trib, g3)
    out = jnp.where(lane < hv, cf, jnp.where(lane < 2 * hv, cb, jax.nn.sigmoid(x)))
    gcol_ref[...] = out
    grow_ref[...] = out.T


def _gates(pg, alog_lane, dtb_lane, tm=256):
    bsz, length, nl = pg.shape
    tm = min(tm, length)
    const = lambda shape: pl.BlockSpec(shape, lambda b, i: (0,) * len(shape))
    return pl.pallas_call(
        _gate_kernel,
        grid=(bsz, length // tm),
        in_specs=[pl.BlockSpec((None, tm, nl), lambda b, i: (b, i, 0)), const((1, nl)), const((1, nl)),
                  const((tm, tm)), const((tm, tm))],
        out_specs=[pl.BlockSpec((None, tm, nl), lambda b, i: (b, i, 0)),
                   pl.BlockSpec((None, nl, tm), lambda b, i: (b, 0, i))],
        out_shape=[jax.ShapeDtypeStruct((bsz, length, nl), F32), jax.ShapeDtypeStruct((bsz, nl, length), F32)],
        compiler_params=_params(("parallel", "parallel")),
        name="gdn_gates",
    )(pg, alog_lane, dtb_lane, _chunk_tri(tm, C_CHUNK, False), _chunk_tri(tm, C_CHUNK, True))


def _qkv_kernel(prev_ref, x_ref, next_ref, cw_ref, shift_ref, o_ref, *, l2, scale):
    i = pl.program_id(1)
    tm = x_ref.shape[0]
    xb = x_ref[...]
    prev8 = jnp.where(i > 0, prev_ref[8:16, :].astype(F32), 0.0)
    next8 = jnp.where(i < pl.num_programs(1) - 1, next_ref[0:8, :].astype(F32), 0.0)
    sub = lax.broadcasted_iota(jnp.int32, (8, xb.shape[1]), 0)
    top8, bot8 = xb[0:8].astype(F32), xb[tm - 8:tm].astype(F32)
    acc = None
    nshift = 0
    for j in range(C_CONV):
        off = j - C_CONV // 2
        if off == 0:
            sh = xb.astype(F32)
        else:
            body = _dot(shift_ref[nshift], xb)
            nshift += 1
            if off < 0:
                fixed = jnp.where(sub < -off, pltpu.roll(prev8, -off, 0), pltpu.roll(top8, -off, 0))
                sh = jnp.concatenate([fixed, body[8:]], axis=0)
            else:
                fixed = jnp.where(sub >= 8 - off, pltpu.roll(next8, 8 - off, 0), pltpu.roll(bot8, 8 - off, 0))
                sh = jnp.concatenate([body[:tm - 8], fixed], axis=0)
        term = cw_ref[j:j + 1, :] * sh
        acc = term if acc is None else acc + term
    y = _silu(acc)
    if l2:
        parts = []
        for h in range(y.shape[1] // HEAD_DIM):
            yh = y[:, h * HEAD_DIM:(h + 1) * HEAD_DIM]
            parts.append(yh * (lax.rsqrt(jnp.sum(yh * yh, axis=-1, keepdims=True) + EPS) * scale))
        y = jnp.concatenate(parts, axis=1)
    o_ref[...] = y.astype(BF16)


def _qkv(p, col0, width, conv_w, l2, scale, tm=256, tc=2048):
    bsz, length, _ = p.shape
    tm = min(tm, length)
    c0 = col0 // tc
    halo = 16
    nsub = length // halo
    t = np.arange(tm)
    offs = [j - C_CONV // 2 for j in range(C_CONV) if j != C_CONV // 2]
    shifts = jnp.asarray(np.stack([t[None, :] == t[:, None] + off for off in offs]), BF16)
    return pl.pallas_call(
        functools.partial(_qkv_kernel, l2=l2, scale=scale),
        grid=(bsz, length // tm, width // tc),
        in_specs=[pl.BlockSpec((None, halo, tc),
                               lambda b, i, c: (b, jnp.maximum(i * (tm // halo) - 1, 0), c0 + c)),
                  pl.BlockSpec((None, tm, tc), lambda b, i, c: (b, i, c0 + c)),
                  pl.BlockSpec((None, halo, tc),
                               lambda b, i, c: (b, jnp.minimum((i + 1) * (tm // halo), nsub - 1), c0 + c)),
                  pl.BlockSpec((C_CONV, tc), lambda b, i, c: (0, c0 + c)),
                  pl.BlockSpec(shifts.shape, lambda b, i, c: (0, 0, 0))],
        out_specs=pl.BlockSpec((None, tm, tc), lambda b, i, c: (b, i, c)),
        out_shape=jax.ShapeDtypeStruct((bsz, length, width), BF16),
        compiler_params=_params(("parallel", "parallel", "parallel")),
        name="gdn_qkv",
    )(p, p, p, conv_w, shifts)


def _gdn_masks():
    t = np.arange(DCHUNK)
    same = (t[:, None] // C_CHUNK) == (t[None, :] // C_CHUNK)
    le, lt = t[None, :] <= t[:, None], t[None, :] < t[:, None]
    masks = [same & le, same & lt, same & le.T, same & lt.T]
    m = 1
    while m < C_CHUNK:
        masks.append(((t[:, None] // (2 * m)) == (t[None, :] // (2 * m))) & ((t[:, None] // m) != (t[None, :] // m)))
        m *= 2
    return jnp.asarray(np.stack(masks), F32)


def _block_diag2(t):
    z = jnp.zeros((t.shape[0], HEAD_DIM), t.dtype)
    return jnp.concatenate([jnp.concatenate([t[:, :HEAD_DIM], z], axis=1),
                            jnp.concatenate([z, t[:, HEAD_DIM:]], axis=1)], axis=0)


def _pair_dot(lhs, rhs):
    h = HEAD_DIM
    return jnp.concatenate([_dot(lhs[:, :h], rhs[:, :h]), _dot(lhs[:, h:], rhs[:, h:])], axis=1)


def _gdn_a_kernel(q_ref, k_ref, v_ref, gcol_ref, grow_ref, mask_ref, lmask_ref, u_ref, w_ref, qk_ref,
                  a_scr, x_scr):
    j = pl.program_id(1)
    hv = gcol_ref.shape[1] // 4
    shift = lax.rem(HEAD_DIM - 2 * j, HEAD_DIM)
    nlvl = lmask_ref.shape[0]
    ndc = q_ref.shape[0] // DCHUNK
    rid = lax.broadcasted_iota(jnp.int32, (DCHUNK, DCHUNK), 0)
    cid = lax.broadcasted_iota(jnp.int32, (DCHUNK, DCHUNK), 1)
    eye = (rid == cid).astype(F32)
    sub8 = lax.broadcasted_iota(jnp.int32, (8, DCHUNK), 0)
    brows, erows = [], []
    for dc in range(ndc):
        rows = slice(dc * DCHUNK, (dc + 1) * DCHUNK)
        q, k = q_ref[rows, :], k_ref[rows, :]
        kk, qk = _dot_nt(k, k), _dot_nt(q, k)
        gc8 = pltpu.roll(gcol_ref[rows, :], shift, 1)

        def gr(lane0):
            g8 = grow_ref[pl.ds(pl.multiple_of(lane0 // 8 * 8 + 8 * (j // 4), 8), 8), rows]
            return jnp.sum(jnp.where(sub8 == lane0 % 8 + 2 * lax.rem(j, 4), g8, 0.0), axis=0, keepdims=True)
        for d in range(2):
            causal, strict = mask_ref[2 * d], mask_ref[2 * d + 1]
            pair_a, pair_x = [], []
            for e in range(2):
                u = 2 * d + e
                gcc = gc8[:, d * hv + e:d * hv + e + 1]
                bcol = gc8[:, (2 + d) * hv + e:(2 + d) * hv + e + 1]
                gcr = gr(d * hv + e)
                dec = jnp.exp(jnp.where(causal > 0, gcc - gcr, 0.0)) * causal
                a = strict * (bcol * kk * dec)
                pair_a.append(a.astype(BF16))
                pair_x.append(eye - a * mask_ref[4])
                qk_ref[u, rows, :] = (qk * dec).astype(BF16)
            a_scr[2 * dc + d] = jnp.concatenate(pair_a, axis=1)
            x_scr[2 * dc + d] = jnp.concatenate(pair_x, axis=1)
            brows.append(jnp.concatenate([gr((2 + d) * hv), gr((2 + d) * hv + 1)], axis=1))
            erows.append(jnp.exp(jnp.concatenate([gr(d * hv), gr(d * hv + 1)], axis=1)))
    for lvl in range(nlvl):
        lm = lmask_ref[lvl]
        xb = [x_scr[c].astype(BF16) for c in range(2 * ndc)]
        pm = [_pair_dot(a_scr[c] * lm, xb[c]).astype(BF16) for c in range(2 * ndc)]
        for c in range(2 * ndc):
            x_scr[c] = x_scr[c] - _pair_dot(xb[c], pm[c])
    for c in range(2 * ndc):
        dc, d = c // 2, c % 2
        rows = slice(dc * DCHUNK, (dc + 1) * DCHUNK)
        mb = x_scr[c] * brows[c]
        k = k_ref[rows, :]
        up = _dot(mb.astype(BF16), _block_diag2(v_ref[rows, :])).astype(BF16)
        wp = _dot((mb * erows[c]).astype(BF16), _block_diag2(jnp.concatenate([k, k], axis=1))).astype(BF16)
        for e in range(2):
            u_ref[2 * d + e, rows, :] = up[:, e * HEAD_DIM:(e + 1) * HEAD_DIM]
            w_ref[2 * d + e, rows, :] = wp[:, e * HEAD_DIM:(e + 1) * HEAD_DIM]


def _gdn_a(qn, kn, vv, gcol, grow, lblk=2048):
    bsz, length, kw = qn.shape
    kh = kw // HEAD_DIM
    lblk = min(lblk, length)
    masks = _gdn_masks()
    lmasks = jnp.concatenate([masks[5:], masks[5:]], axis=2).astype(BF16)
    masks = masks[:5]
    npair = 2 * (lblk // DCHUNK)
    unit = jax.ShapeDtypeStruct((bsz, kh, 4, length, HEAD_DIM), BF16)
    unit_spec = pl.BlockSpec((None, None, 4, lblk, HEAD_DIM), lambda b, j, n: (b, j, 0, n, 0))
    return pl.pallas_call(
        _gdn_a_kernel,
        grid=(bsz, kh, length // lblk),
        in_specs=[pl.BlockSpec((None, lblk, HEAD_DIM), lambda b, j, n: (b, n, j)),
                  pl.BlockSpec((None, lblk, HEAD_DIM), lambda b, j, n: (b, n, j)),
                  pl.BlockSpec((None, lblk, 2 * HEAD_DIM), lambda b, j, n: (b, n, j)),
                  pl.BlockSpec((None, lblk, HEAD_DIM), lambda b, j, n: (b, n, 0)),
                  pl.BlockSpec((None, grow.shape[1], lblk), lambda b, j, n: (b, 0, n)),
                  pl.BlockSpec(masks.shape, lambda b, j, n: (0, 0, 0)),
                  pl.BlockSpec(lmasks.shape, lambda b, j, n: (0, 0, 0))],
        out_specs=[unit_spec, unit_spec, unit_spec],
        out_shape=[unit, unit, unit],
        scratch_shapes=[pltpu.VMEM((npair, DCHUNK, 2 * HEAD_DIM), BF16),
                        pltpu.VMEM((npair, DCHUNK, 2 * HEAD_DIM), F32)],
        compiler_params=_params(("parallel", "parallel", "parallel")),
        name="gdn_local",
    )(qn, kn, vv, gcol, grow, masks, lmasks)


def _gdn_b_kernel(*refs):
    nin = 6
    s0_ref, of_ref, ob_ref, sfin_ref, st_scr = refs[2 * nin:]
    n = pl.program_id(2)
    kb = st_scr.shape[0]

    @pl.when(n == 0)
    def _():
        st_scr[...] = s0_ref[...]

    zeros = jnp.zeros((C_CHUNK, 2 * HEAD_DIM), BF16)
    o_refs = (of_ref, ob_ref)
    nchunk = refs[0].shape[0] // C_CHUNK
    chains = [(jj, d) for jj in range(kb) for d in range(2)]
    pair = lambda ref, jj, r: jnp.concatenate([ref[jj, 0, r, :], ref[jj, 1, r, :]], axis=1)
    gc8 = {}
    hv = refs[5].shape[1] // 4
    for jj, d in chains:
        shift = lax.rem(HEAD_DIM - 2 * (pl.program_id(1) * kb + jj), HEAD_DIM)
        gc8[jj, d] = pltpu.roll(refs[nin * d + 5][...], shift, 1)
    for step in range(nchunk):
        vns, qss, ekds, gls = {}, {}, {}, {}
        for jj, d in chains:
            q_ref, _, u_ref, w_ref, _, _ = refs[nin * d:nin * d + nin]
            ci = step if d == 0 else nchunk - 1 - step
            r = slice(ci * C_CHUNK, (ci + 1) * C_CHUNK)
            last = (ci + 1) * C_CHUNK - 1 if d == 0 else ci * C_CHUNK
            qf = q_ref[r, jj * HEAD_DIM:(jj + 1) * HEAD_DIM].astype(F32)
            qd, ekd, gl = [], [], []
            for e in range(2):
                gcc = gc8[jj, d][r, d * hv + e:d * hv + e + 1]
                glast = gc8[jj, d][last:last + 1, d * hv + e:d * hv + e + 1]
                qd.append((qf * jnp.exp(gcc)).astype(BF16))
                ekd.append(jnp.exp(glast - gcc))
                gl.append(jnp.broadcast_to(jnp.exp(glast), (1, HEAD_DIM)))
            lhs = jnp.concatenate([pair(w_ref, jj, r), jnp.concatenate(qd, axis=1)], axis=0)
            res = _dot(lhs, _block_diag2(st_scr[jj, d].astype(BF16)))
            vns[jj, d] = pair(u_ref, jj, r).astype(F32) - res[:C_CHUNK]
            qss[jj, d], ekds[jj, d], gls[jj, d] = res[C_CHUNK:], ekd, jnp.concatenate(gl, axis=1)
        for jj, d in chains:
            _, k_ref, _, _, qk_ref, _ = refs[nin * d:nin * d + nin]
            ci = step if d == 0 else nchunk - 1 - step
            r = slice(ci * C_CHUNK, (ci + 1) * C_CHUNK)
            vn = vns[jj, d]
            vnb = vn.astype(BF16)
            placed = jnp.concatenate([vnb, zeros] if ci % 2 == 0 else [zeros, vnb], axis=0)
            o = qss[jj, d] + _dot(pair(qk_ref, jj, r), _block_diag2(placed))
            o_refs[d][r, 2 * jj * HEAD_DIM:2 * (jj + 1) * HEAD_DIM] = o.astype(BF16)
            vs = jnp.concatenate([vn[:, :HEAD_DIM] * ekds[jj, d][0], vn[:, HEAD_DIM:] * ekds[jj, d][1]], axis=1)
            upd = _dot_tn(k_ref[r, jj * HEAD_DIM:(jj + 1) * HEAD_DIM], vs.astype(BF16))
            st_scr[jj, d] = st_scr[jj, d] * gls[jj, d] + upd

    @pl.when(n == pl.num_programs(2) - 1)
    def _():
        sfin_ref[...] = st_scr[...]


def _gdn_b(qn, kn, uu, ww, qk, gcol, s0, lblk=512, kb=4):
    bsz, length, kw = kn.shape
    kh = kw // HEAD_DIM
    lblk = min(lblk, length)
    nb = length // lblk

    def specs(d):
        blk = (lambda n: nb - 1 - n) if d else (lambda n: n)
        unit = pl.BlockSpec((None, kb, 2, lblk, HEAD_DIM), lambda b, j, n: (b, j, d, blk(n), 0))
        return [pl.BlockSpec((None, lblk, kb * HEAD_DIM), lambda b, j, n: (b, blk(n), j)),
                pl.BlockSpec((None, lblk, kb * HEAD_DIM), lambda b, j, n: (b, blk(n), j)),
                unit, unit, unit,
                pl.BlockSpec((None, lblk, gcol.shape[2]), lambda b, j, n: (b, blk(n), 0))]

    st_spec = pl.BlockSpec((None, kb, 2, HEAD_DIM, 2 * HEAD_DIM), lambda b, j, n: (b, j, 0, 0, 0))
    o_shape = jax.ShapeDtypeStruct((bsz, length, 2 * kw), BF16)
    args = [qn, kn, uu, ww, qk, gcol]
    return pl.pallas_call(
        _gdn_b_kernel,
        grid=(bsz, kh // kb, nb),
        in_specs=specs(0) + specs(1) + [st_spec],
        out_specs=[pl.BlockSpec((None, lblk, 2 * kb * HEAD_DIM), lambda b, j, n: (b, n, j)),
                   pl.BlockSpec((None, lblk, 2 * kb * HEAD_DIM), lambda b, j, n: (b, nb - 1 - n, j)),
                   st_spec],
        out_shape=[o_shape, o_shape, jax.ShapeDtypeStruct(s0.shape, F32)],
        scratch_shapes=[pltpu.VMEM((kb, 2, HEAD_DIM, 2 * HEAD_DIM), F32)],
        compiler_params=_params(("parallel", "parallel", "arbitrary")),
        name="gdn_scan",
    )(*args, *args, s0)


def kernel(x, c, ctx, c_ctx, w_ada, b_ada, norm_w, ev_w_in, ev_lb, ev_a_norm, ev_pool_w, ev_pool_scale,
           ev_w_out, od_w_in, od_conv, od_A_log, od_dt_bias, od_norm, od_w_out, ffn_w13, ffn_w2):
    bsz, seq, d = x.shape
    depth = w_ada.shape[0]
    ctx_len = ctx.shape[1]
    a_width = ev_lb.shape[2]
    heads_a = a_width // HEAD_DIM
    assert bsz + 1 <= 8

    cvec = jnp.zeros((8, d), F32).at[0].set(c_ctx).at[1:1 + bsz].set(c)
    mod = _ada(cvec, w_ada, b_ada).reshape(depth, 8, 6, d)
    lat_row = lambda b: b + 1
    ctx_row = lambda b: 0
    flat = lambda t: t.reshape(1, bsz * t.shape[1], t.shape[2])
    unflat = lambda t: t.reshape(bsz, t.shape[1] // bsz, t.shape[2])
    ev_w_in_b, ev_w_out_b, od_w_in_b, od_w_out_b, ffn_w13_b, ffn_w2_b = (
        t.astype(BF16) for t in (ev_w_in, ev_w_out, od_w_in, od_w_out, ffn_w13, ffn_w2))
    lb_all = jnp.cumsum(jax.nn.softmax(ev_lb.astype(F32), axis=1), axis=1)

    for layer in range(depth):
        need_ctx = layer < depth - 1
        j = layer // 2
        nw = norm_w[layer]
        if layer % 2 == 0:
            lb = lb_all[:, layer]
            p_l = _proj(x, mod, layer, lat_row, nw[0], ev_w_in_b, j, ev_w_in.shape[2])
            p_c = unflat(_proj(flat(ctx), mod, layer, ctx_row, nw[0], ev_w_in_b, j, ev_w_in.shape[2]))
            s0 = jnp.zeros((2, bsz, heads_a, HEAD_DIM, HEAD_DIM), F32)
            oc_f, oc_b, s_ctx = _hgrn(p_c, lb, s0)
            ol_f, ol_b, _ = _hgrn(p_l, lb, s_ctx)
            x = _even_out(ol_f, ol_b, p_l, x, mod, layer, lat_row, ev_a_norm[j], ev_pool_w[j],
                          ev_pool_scale[j], ev_w_out_b, j, nw[1], GRID_W)
            if need_ctx:
                ctx = _even_out(oc_f, oc_b, p_c, ctx, mod, layer, ctx_row, ev_a_norm[j], ev_pool_w[j],
                                ev_pool_scale[j], ev_w_out_b, j, nw[1], ctx_len)
        else:
            assert not need_ctx
            hv = od_A_log.shape[2]
            kh = hv // 2
            kw, vw = kh * HEAD_DIM, hv * HEAD_DIM
            nmain = 2 * kw + 2 * vw
            wg = _split_bf16(od_w_in[j][:, nmain:])
            lane_param = lambda t: jnp.concatenate([t.reshape(1, -1), jnp.zeros((1, 2 * hv), F32)], axis=1)
            alog_lane, dtb_lane = lane_param(od_A_log[j]), lane_param(od_dt_bias[j])
            conv_w = od_conv[j]

            def mix(p, pg, s0):
                gcol, grow = _gates(pg, alog_lane, dtb_lane)
                qn = _qkv(p, 0, kw, conv_w, True, HEAD_DIM ** -0.5)
                kn = _qkv(p, kw, kw, conv_w, True, 1.0)
                vv = _qkv(p, 2 * kw, vw, conv_w, False, 1.0)
                uu, ww, qk = _gdn_a(qn, kn, vv, gcol, grow)
                return _gdn_b(qn, kn, uu, ww, qk, gcol, s0)

            p_l, g_l = _proj(x, mod, layer, lat_row, nw[0], od_w_in_b, j, nmain, wg)
            p_c, g_c = (unflat(t) for t in _proj(flat(ctx), mod, layer, ctx_row, nw[0], od_w_in_b, j, nmain, wg))
            s0 = jnp.zeros((bsz, kh, 2, HEAD_DIM, 2 * HEAD_DIM), F32)
            _, _, s_ctx = mix(p_c, g_c, s0)
            o_f, o_b, _ = mix(p_l, g_l, s_ctx)
            x = _odd_out(o_f, o_b, p_l, 2 * kw + vw, x, mod, layer, lat_row, od_norm[j], od_w_out_b, j, nw[1])
        x = _ffn(x, mod, layer, lat_row, nw[2], nw[3], ffn_w13_b, ffn_w2_b)
        if need_ctx:
            ctx = unflat(_ffn(flat(ctx), mod, layer, ctx_row, nw[2], nw[3], ffn_w13_b, ffn_w2_b))
    return x
```
